```python
import math
import jax
import jax.numpy as jnp
from jax import lax
import numpy as np

D_MODEL = 1024
BATCH = 8
SEQ = 4096
DEPTH = 1
DEC_BATCH = 8
DEC_SEQ = 64
PAST_LEN = 1024

CHUNK = 64
Q_BLOCK = 128
ROPE_THETA = 500000.0
EPS = 1e-6
H_A = 8
Q_LORA = 384
KV_LORA = 256
QK_NOPE = 64
QK_ROPE = 32
V_A = 128
MLA_SCALE = 1.0 / math.sqrt(QK_NOPE + QK_ROPE)
H_B = 8
D_B = 64
V_B = 2 * D_B
ROT_B = D_B // 4
DIFF_SCALE = 1.0 / math.sqrt(D_B)
N_GROUPS = 4
EXP_PER_GROUP = 8
N_EXPERTS = N_GROUPS * EXP_PER_GROUP
TOP_K_IN_GROUP = 2
D_EXPERT = 256
D_PLE = 256
IN_SIZES = (Q_LORA, KV_LORA, QK_ROPE, H_B * 2 * D_B, H_B * 2 * D_B, H_B * V_B, D_MODEL, D_MODEL)
IN_COLS = sum(IN_SIZES)
IN_OFFSETS = tuple(sum(IN_SIZES[:i + 1]) for i in range(len(IN_SIZES) - 1))

kernel_name = 'hybrid_mla_diffattn_hmoe_stream_step'


def _rms(x, g):
    xf = x.astype(jnp.float32)
    y = xf * lax.rsqrt(jnp.mean(xf * xf, axis=-1, keepdims=True) + EPS)
    return (y * g.astype(jnp.float32)).astype(x.dtype)


def _rope(x, pos):
    r = x.shape[-1]
    half = r // 2
    inv = ROPE_THETA ** (-jnp.arange(half, dtype=jnp.float32) * (2.0 / r))
    ang = pos.astype(jnp.float32)[:, None] * inv[None, :]
    ang = ang.reshape((1, pos.shape[0]) + (1,) * (x.ndim - 3) + (half,))
    c, s = jnp.cos(ang), jnp.sin(ang)
    xf = x.astype(jnp.float32)
    x1, x2 = xf[..., :half], xf[..., half:]
    return jnp.concatenate([x1 * c - x2 * s, x2 * c + x1 * s], axis=-1).astype(x.dtype)


def _partial_rope(x, pos, rot):
    return jnp.concatenate([_rope(x[..., :rot], pos), x[..., rot:]], axis=-1)


def _chunk_mask(q_pos, k_pos):
    return (k_pos[None, :] // CHUNK) <= (q_pos[:, None] // CHUNK)


def _over_query_blocks(fn, qs, q_pos):
    sq = q_pos.shape[0]
    if sq <= Q_BLOCK or sq % Q_BLOCK != 0:
        return fn(qs, q_pos)
    nb = sq // Q_BLOCK
    qs_b = tuple(jnp.moveaxis(q.reshape((q.shape[0], nb, Q_BLOCK) + q.shape[2:]), 1, 0) for q in qs)
    out = lax.map(lambda a: fn(a[0], a[1]), (qs_b, q_pos.reshape(nb, Q_BLOCK)))
    out = jnp.moveaxis(out, 0, 1)
    return out.reshape((out.shape[0], sq) + out.shape[3:])


def _hier_moe(x, w_rg, b_rg, w_re, b_re, w_ei, w_eo):
    n = x.shape[0]
    p_grp = jax.nn.softmax((x @ w_rg).astype(jnp.float32) + b_rg.astype(jnp.float32), axis=-1)
    p_top, g_sel = lax.top_k(p_grp, 1)
    le = ((x @ w_re).astype(jnp.float32) + b_re.astype(jnp.float32)).reshape(n, N_GROUPS, EXP_PER_GROUP)
    le = jnp.take_along_axis(le, g_sel[:, :, None], axis=1)[:, 0]
    v_top, i_top = lax.top_k(le, TOP_K_IN_GROUP)
    w_top = jax.nn.softmax(v_top, axis=-1) * p_top
    e_id = g_sel * EXP_PER_GROUP + i_top
    gate = jnp.sum(jax.nn.one_hot(e_id, N_EXPERTS, dtype=jnp.float32) * w_top[..., None], axis=1).astype(x.dtype)
    y = jnp.zeros_like(x)
    for e in range(N_EXPERTS):
        a, b = jnp.split(x @ w_ei[e], 2, axis=-1)
        y = y + gate[:, e:e + 1] * ((jax.nn.silu(a) * b) @ w_eo[e])
    return y


def _layer(x, pe, pos, past, w, lam_init):
    b, s, _ = x.shape
    xn = _rms(x, w['g_mix'])
    z = xn @ w['w_in']
    q_lat, c_kv, kr_raw, dq, dk, dv, ga, gb = jnp.split(z, IN_OFFSETS, axis=-1)

    q = (_rms(q_lat, w['g_q_lat']) @ w['w_uq']).reshape(b, s, H_A, QK_NOPE + QK_ROPE)
    q_nope = _rms(q[..., :QK_NOPE], w['g_mla_qn'])
    q_rope = _rope(_rms(q[..., QK_NOPE:], w['g_mla_qr']), pos)
    ckv = _rms(c_kv, w['g_kv_lat'])
    krope = _rope(_rms(kr_raw, w['g_mla_kr']), pos)

    dq = _partial_rope(_rms(dq.reshape(b, s, H_B, 2, D_B), w['g_diff_q']), pos, ROT_B)
    dk = _partial_rope(_rms(dk.reshape(b, s, H_B, 2, D_B), w['g_diff_k']), pos, ROT_B)
    dv = dv.reshape(b, s, H_B, V_B)

    if past is None:
        ckv_all, kr_all, dk_all, dv_all, k_pos = ckv, krope, dk, dv, pos
    else:
        ckv_p, kr_p, dk_p, dv_p = past
        ckv_all = jnp.concatenate([ckv_p, ckv], axis=1)
        kr_all = jnp.concatenate([kr_p, krope], axis=1)
        dk_all = jnp.concatenate([dk_p, dk], axis=1)
        dv_all = jnp.concatenate([dv_p, dv], axis=1)
        k_pos = jnp.concatenate([jnp.arange(ckv_p.shape[1], dtype=jnp.int32), pos])

    k_nope = _rms((ckv_all @ w['w_uk']).reshape(b, -1, H_A, QK_NOPE), w['g_mla_kn'])
    v_a = (ckv_all @ w['w_uv']).reshape(b, -1, H_A, V_A)

    def mla_fn(qs, qp):
        qn, qr = qs
        sc = (jnp.einsum('bqhd,bkhd->bhqk', qn, k_nope)
              + jnp.einsum('bqhr,bkr->bhqk', qr, kr_all)).astype(jnp.float32) * MLA_SCALE
        sc = jnp.where(_chunk_mask(qp, k_pos), sc, -jnp.inf)
        pr = jax.nn.softmax(sc, axis=-1).astype(v_a.dtype)
        return jnp.einsum('bhqk,bkhd->bqhd', pr, v_a)

    o_a = _over_query_blocks(mla_fn, (q_nope, q_rope), pos).reshape(b, s, H_A * V_A)

    lam = (jnp.exp(jnp.sum(w['lambda_q1'].astype(jnp.float32) * w['lambda_k1'].astype(jnp.float32)))
           - jnp.exp(jnp.sum(w['lambda_q2'].astype(jnp.float32) * w['lambda_k2'].astype(jnp.float32)))
           + lam_init)

    def diff_fn(qs, qp):
        (qb,) = qs
        sc = jnp.einsum('bqhcd,bkhcd->bchqk', qb, dk_all).astype(jnp.float32) * DIFF_SCALE
        sc = jnp.where(_chunk_mask(qp, k_pos), sc, -jnp.inf)
        pr = jax.nn.softmax(sc, axis=-1)
        att = (pr[:, 0] - lam * pr[:, 1]).astype(dv_all.dtype)
        return jnp.einsum('bhqk,bkhd->bqhd', att, dv_all)

    o_b = _over_query_blocks(diff_fn, (dq,), pos)
    o_b = (_rms(o_b, w['g_diff_sub']) * (1.0 - lam_init)).reshape(b, s, H_B * V_B)

    merged = jax.nn.sigmoid(ga) * o_a + jax.nn.sigmoid(gb) * o_b
    h = x + merged @ w['w_out']

    hn = _rms(h, w['g_ffn']).reshape(b * s, D_MODEL)
    h = h + _hier_moe(hn, w['w_router_grp'], w['b_router_grp'], w['w_router_exp'], w['b_router_exp'],
                      w['w_exp_in'], w['w_exp_out']).reshape(b, s, D_MODEL)

    h = h + jax.nn.sigmoid(_rms(h, w['g_ple']) @ w['w_ple_gate']) * (pe @ w['w_ple_proj'])
    return h, (ckv, krope, dk, dv)


def setup_inputs(seed: int = 0) -> dict:
    key = jax.random.key(seed)
    ks = iter(jax.random.split(key, 64))
    f32 = jnp.float32

    def nrm(shape, fan_in=None, scale=None):
        sc = scale if scale is not None else fan_in ** -0.5
        return jax.random.normal(next(ks), shape, f32) * sc

    def gain(shape):
        return 1.0 + 0.02 * jax.random.normal(next(ks), shape, f32)

    L = DEPTH
    return {
        'x_prompt': nrm((BATCH, SEQ, D_MODEL), scale=1.0),
        'x_sample': nrm((DEC_BATCH, DEC_SEQ, D_MODEL), scale=1.0),
        'p_prompt': nrm((L, BATCH, SEQ, D_PLE), scale=1.0),
        'p_sample': nrm((L, DEC_BATCH, DEC_SEQ, D_PLE), scale=1.0),
        'cache_mla_ckv': nrm((L, DEC_BATCH, PAST_LEN, KV_LORA), scale=1.0),
        'cache_mla_krope': nrm((L, DEC_BATCH, PAST_LEN, QK_ROPE), scale=1.0),
        'cache_diff_k': nrm((L, DEC_BATCH, PAST_LEN, H_B, 2, D_B), scale=1.0),
        'cache_diff_v': nrm((L, DEC_BATCH, PAST_LEN, H_B, V_B), scale=1.0),
        'g_mix': gain((L, D_MODEL)),
        'w_in': nrm((L, D_MODEL, IN_COLS), D_MODEL),
        'g_q_lat': gain((L, Q_LORA)),
        'w_uq': nrm((L, Q_LORA, H_A * (QK_NOPE + QK_ROPE)), Q_LORA),
        'g_kv_lat': gain((L, KV_LORA)),
        'w_uk': nrm((L, KV_LORA, H_A * QK_NOPE), KV_LORA),
        'w_uv': nrm((L, KV_LORA, H_A * V_A), KV_LORA),
        'g_mla_qn': gain((L, QK_NOPE)),
        'g_mla_qr': gain((L, QK_ROPE)),
        'g_mla_kn': gain((L, QK_NOPE)),
        'g_mla_kr': gain((L, QK_ROPE)),
        'g_diff_q': gain((L, D_B)),
        'g_diff_k': gain((L, D_B)),
        'lambda_q1': nrm((L, D_B), scale=0.1),
        'lambda_k1': nrm((L, D_B), scale=0.1),
        'lambda_q2': nrm((L, D_B), scale=0.1),
        'lambda_k2': nrm((L, D_B), scale=0.1),
        'g_diff_sub': gain((L, V_B)),
        'w_out': nrm((L, D_MODEL, D_MODEL), D_MODEL),
        'g_ffn': gain((L, D_MODEL)),
        'w_router_grp': nrm((L, D_MODEL, N_GROUPS), D_MODEL),
        'b_router_grp': nrm((L, N_GROUPS), scale=0.01),
        'w_router_exp': nrm((L, D_MODEL, N_EXPERTS), D_MODEL),
        'b_router_exp': nrm((L, N_EXPERTS), scale=0.01),
        'w_exp_in': nrm((L, N_EXPERTS, D_MODEL, 2 * D_EXPERT), D_MODEL),
        'w_exp_out': nrm((L, N_EXPERTS, D_EXPERT, D_MODEL), D_EXPERT),
        'g_ple': gain((L, D_MODEL)),
        'w_ple_gate': nrm((L, D_MODEL, D_MODEL), D_MODEL),
        'w_ple_proj': nrm((L, D_PLE, D_MODEL), D_PLE),
    }


def reference(x_prompt, x_sample, p_prompt, p_sample, cache_mla_ckv, cache_mla_krope, cache_diff_k, cache_diff_v,
              g_mix, w_in, g_q_lat, w_uq, g_kv_lat, w_uk, w_uv, g_mla_qn, g_mla_qr, g_mla_kn, g_mla_kr,
              g_diff_q, g_diff_k, lambda_q1, lambda_k1, lambda_q2, lambda_k2, g_diff_sub, w_out,
              g_ffn, w_router_grp, b_router_grp, w_router_exp, b_router_exp, w_exp_in, w_exp_out,
              g_ple, w_ple_gate, w_ple_proj):
    pos_p = jnp.arange(x_prompt.shape[1], dtype=jnp.int32)
    pos_s = cache_mla_ckv.shape[2] + jnp.arange(x_sample.shape[1], dtype=jnp.int32)
    hp, hs = x_prompt, x_sample
    ckv_p, kr_p, dk_p, dv_p = [], [], [], []
    ckv_s, kr_s, dk_s, dv_s = [], [], [], []
    for l in range(DEPTH):
        w = {
            'g_mix': g_mix[l], 'w_in': w_in[l], 'g_q_lat': g_q_lat[l], 'w_uq': w_uq[l],
            'g_kv_lat': g_kv_lat[l], 'w_uk': w_uk[l], 'w_uv': w_uv[l],
            'g_mla_qn': g_mla_qn[l], 'g_mla_qr': g_mla_qr[l], 'g_mla_kn': g_mla_kn[l], 'g_mla_kr': g_mla_kr[l],
            'g_diff_q': g_diff_q[l], 'g_diff_k': g_diff_k[l],
            'lambda_q1': lambda_q1[l], 'lambda_k1': lambda_k1[l], 'lambda_q2': lambda_q2[l], 'lambda_k2': lambda_k2[l],
            'g_diff_sub': g_diff_sub[l], 'w_out': w_out[l], 'g_ffn': g_ffn[l],
            'w_router_grp': w_router_grp[l], 'b_router_grp': b_router_grp[l],
            'w_router_exp': w_router_exp[l], 'b_router_exp': b_router_exp[l],
            'w_exp_in': w_exp_in[l], 'w_exp_out': w_exp_out[l],
            'g_ple': g_ple[l], 'w_ple_gate': w_ple_gate[l], 'w_ple_proj': w_ple_proj[l],
        }
        lam_init = 0.8 - 0.6 * math.exp(-0.3 * l)
        hp, st_p = _layer(hp, p_prompt[l], pos_p, None, w, lam_init)
        hs, st_s = _layer(hs, p_sample[l], pos_s,
                          (cache_mla_ckv[l], cache_mla_krope[l], cache_diff_k[l], cache_diff_v[l]), w, lam_init)
        ckv_p.append(st_p[0]); kr_p.append(st_p[1]); dk_p.append(st_p[2]); dv_p.append(st_p[3])
        ckv_s.append(st_s[0]); kr_s.append(st_s[1]); dk_s.append(st_s[2]); dv_s.append(st_s[3])
    return (hp, hs,
            jnp.stack(ckv_p), jnp.stack(kr_p), jnp.stack(dk_p), jnp.stack(dv_p),
            jnp.stack(ckv_s), jnp.stack(kr_s), jnp.stack(dk_s), jnp.stack(dv_s))
```

```python
import functools
import math

import jax
import jax.numpy as jnp
from jax import lax
from jax.experimental import pallas as pl
from jax.experimental.pallas import tpu as pltpu

F32 = jnp.float32
BF16 = jnp.bfloat16

D_MODEL = 1024
CHUNK = 64
ROPE_THETA = 500000.0
EPS = 1e-6
H_A = 8
Q_LORA = 384
KV_LORA = 256
QK_NOPE = 64
QK_ROPE = 32
V_A = 128
MLA_SCALE = 1.0 / math.sqrt(QK_NOPE + QK_ROPE)
H_B = 8
D_B = 64
V_B = 2 * D_B
ROT_B = D_B // 4
DIFF_SCALE = 1.0 / math.sqrt(D_B)
N_GROUPS = 4
EXP_PER_GROUP = 8
N_EXPERTS = N_GROUPS * EXP_PER_GROUP
D_EXPERT = 256
D_PLE = 256
IN_SIZES = (Q_LORA, KV_LORA, QK_ROPE, H_B * 2 * D_B, H_B * 2 * D_B, H_B * V_B, D_MODEL, D_MODEL)

LANES = 128
N_HEADS = 8
VMEM_LIMIT = 56 * 1024 * 1024

_O_Q = 0
_O_CKV = _O_Q + Q_LORA
_O_KR = _O_CKV + KV_LORA
_O_DQ = _O_KR + LANES
_O_DK = _O_DQ + D_MODEL
_O_DV = _O_DK + D_MODEL
_O_GA = _O_DV + D_MODEL
_O_GB = _O_GA + D_MODEL
_W_MAIN_COLS = _O_GB + D_MODEL


def _dot(a, b):
    return jnp.dot(a, b, preferred_element_type=F32)


def _sigmoid(x):
    return 1.0 / (1.0 + jnp.exp(-x))


def _row_rms(x, g):
    ms = jnp.mean(x * x, axis=-1, keepdims=True)
    return (x * lax.rsqrt(ms + EPS)) * g


def _seg_rms(z, bd, g):
    ms = _dot((z * z).astype(BF16), bd)
    return (z * lax.rsqrt(ms + EPS)) * g


def _rope_block(y, tab_ref, half):
    return (y * tab_ref[0]
            + pltpu.roll(y, LANES - half, 1) * tab_ref[1]
            + pltpu.roll(y, half, 1) * tab_ref[2])


def _proj_kernel(x_ref, taba_ref, tabb_ref, gmix_ref, w_ref, gql_ref, wuq_ref, gkv_ref, gv_ref, bd_ref,
                 qcat_ref, ckv_ref, kr_ref, dq_ref, dk32_ref, dk16_ref, dv32_ref, dv16_ref, ga_ref, gb_ref):
    xn = _row_rms(x_ref[...], gmix_ref[...]).astype(BF16)
    bd_a = bd_ref[0]
    bd_b = bd_ref[1]

    ql = _row_rms(_dot(xn, w_ref[:, _O_Q:_O_Q + Q_LORA]), gql_ref[...]).astype(BF16)
    q = _dot(ql, wuq_ref[...])
    for h in range(N_HEADS):
        sl = slice(h * LANES, (h + 1) * LANES)
        qb = _rope_block(_seg_rms(q[:, sl], bd_a, gv_ref[0:1, :]), taba_ref, QK_ROPE // 2)
        qcat_ref[:, sl] = (qb * MLA_SCALE).astype(BF16)

    ckv_ref[...] = _row_rms(_dot(xn, w_ref[:, _O_CKV:_O_CKV + KV_LORA]), gkv_ref[...])
    kr = _dot(xn, w_ref[:, _O_KR:_O_KR + LANES])
    kr_ref[...] = _rope_block(_seg_rms(kr, bd_a, gv_ref[1:2, :]), taba_ref, QK_ROPE // 2)

    zq = _dot(xn, w_ref[:, _O_DQ:_O_DQ + D_MODEL])
    zk = _dot(xn, w_ref[:, _O_DK:_O_DK + D_MODEL])
    for h in range(N_HEADS):
        sl = slice(h * LANES, (h + 1) * LANES)
        qb = _rope_block(_seg_rms(zq[:, sl], bd_b, gv_ref[2:3, :]), tabb_ref, ROT_B // 2)
        dq_ref[:, sl] = (qb * DIFF_SCALE).astype(BF16)
        kb = _rope_block(_seg_rms(zk[:, sl], bd_b, gv_ref[3:4, :]), tabb_ref, ROT_B // 2)
        dk32_ref[:, sl] = kb
        dk16_ref[:, sl] = kb.astype(BF16)

    dv = _dot(xn, w_ref[:, _O_DV:_O_DV + D_MODEL])
    dv32_ref[...] = dv
    dv16_ref[...] = dv.astype(BF16)
    ga_ref[...] = _sigmoid(_dot(xn, w_ref[:, _O_GA:_O_GA + D_MODEL])).astype(BF16)
    gb_ref[...] = _sigmoid(_dot(xn, w_ref[:, _O_GB:_O_GB + D_MODEL])).astype(BF16)


def _const_spec(shape):
    nd = len(shape)
    return pl.BlockSpec(shape, lambda *_: (0,) * nd, pipeline_mode=pl.Buffered(1))


def _proj(x, tab_a, tab_b, wp):
    b, s, _ = x.shape
    tm = min(256, s)
    grid = (b, s // tm)

    def tok(width):
        return pl.BlockSpec((None, tm, width), lambda bi, i: (bi, i, 0))

    tab_spec = pl.BlockSpec((3, tm, LANES), lambda bi, i: (0, i, 0))
    out_shapes = [
        jax.ShapeDtypeStruct((b, s, D_MODEL), BF16),
        jax.ShapeDtypeStruct((b, s, KV_LORA), F32),
        jax.ShapeDtypeStruct((b, s, LANES), F32),
        jax.ShapeDtypeStruct((b, s, D_MODEL), BF16),
        jax.ShapeDtypeStruct((b, s, D_MODEL), F32),
        jax.ShapeDtypeStruct((b, s, D_MODEL), BF16),
        jax.ShapeDtypeStruct((b, s, D_MODEL), F32),
        jax.ShapeDtypeStruct((b, s, D_MODEL), BF16),
        jax.ShapeDtypeStruct((b, s, D_MODEL), BF16),
        jax.ShapeDtypeStruct((b, s, D_MODEL), BF16),
    ]
    out_specs = [tok(D_MODEL), tok(KV_LORA), tok(LANES), tok(D_MODEL), tok(D_MODEL), tok(D_MODEL),
                 tok(D_MODEL), tok(D_MODEL), tok(D_MODEL), tok(D_MODEL)]
    in_specs = [
        tok(D_MODEL), tab_spec, tab_spec,
        _const_spec((1, D_MODEL)), _const_spec((D_MODEL, _W_MAIN_COLS)),
        _const_spec((1, Q_LORA)), _const_spec((Q_LORA, D_MODEL)), _const_spec((1, KV_LORA)),
        _const_spec((8, LANES)), _const_spec((2, LANES, LANES)),
    ]
    return pl.pallas_call(
        _proj_kernel, grid=grid, in_specs=in_specs, out_specs=out_specs, out_shape=out_shapes,
        compiler_params=pltpu.CompilerParams(dimension_semantics=("parallel", "parallel"),
                                             vmem_limit_bytes=VMEM_LIMIT),
        name="proj",
    )(x, tab_a, tab_b, wp["g_mix"], wp["w_main"], wp["g_q_lat"], wp["w_uq"], wp["g_kv_lat"],
      wp["gvecs"], wp["bd"])


def _kvup_kernel(ckv_ref, kr_ref, wuk_ref, wuv_ref, gv_ref, bd_ref, kcat_ref, va_ref):
    c = ckv_ref[...].astype(BF16)
    kn = _dot(c, wuk_ref[...])
    kr = kr_ref[...]
    bd_a = bd_ref[0]
    for h in range(N_HEADS):
        sl = slice(h * LANES, (h + 1) * LANES)
        kcat_ref[:, sl] = (_seg_rms(kn[:, sl], bd_a, gv_ref[4:5, :]) + kr).astype(BF16)
    va_ref[...] = _dot(c, wuv_ref[...]).astype(BF16)


def _kv_up(ckv, kr, wp):
    m = ckv.shape[0]
    tm = 256
    assert m % tm == 0

    def tok(width):
        return pl.BlockSpec((tm, width), lambda i: (i, 0))

    return pl.pallas_call(
        _kvup_kernel, grid=(m // tm,),
        in_specs=[tok(KV_LORA), tok(LANES), _const_spec((KV_LORA, D_MODEL)), _const_spec((KV_LORA, D_MODEL)),
                  _const_spec((8, LANES)), _const_spec((2, LANES, LANES))],
        out_specs=[tok(D_MODEL), tok(D_MODEL)],
        out_shape=[jax.ShapeDtypeStruct((m, D_MODEL), BF16), jax.ShapeDtypeStruct((m, D_MODEL), BF16)],
        compiler_params=pltpu.CompilerParams(dimension_semantics=("parallel",), vmem_limit_bytes=VMEM_LIMIT),
        name="kv_up",
    )(ckv, kr, wp["w_uk"], wp["w_uv"], wp["gvecs"], wp["bd"])


def _num_kv_blocks(qi, tq, tk, q_off, sk):
    last_row = q_off + (qi + 1) * tq - 1
    k_end = ((last_row >> 6) + 1) * CHUNK
    return jnp.minimum((k_end + tk - 1) // tk, sk // tk)


def _attn_kernel(lamv_ref, gsub_ref, q_ref, k_ref, v_ref, o_ref, m_sc, l_sc, acc_sc,
                 *, maps, tq, tk, q_off, sk, lam_init):
    qi = pl.program_id(1)
    kj = pl.program_id(2)
    rows = maps * tq
    nk = _num_kv_blocks(qi, tq, tk, q_off, sk)
    first_row = q_off + qi * tq
    full_vis_end = ((first_row >> 6) + 1) * CHUNK
    needs_mask = (kj + 1) * tk > full_vis_end
    active = kj < nk

    @pl.when(kj == 0)
    def _init():
        m_sc[...] = jnp.full(m_sc.shape, -jnp.inf, F32)
        l_sc[...] = jnp.zeros(l_sc.shape, F32)
        acc_sc[...] = jnp.zeros(acc_sc.shape, F32)

    def body(masked):
        if masked:
            r_io = lax.broadcasted_iota(jnp.int32, (rows, tk), 0)
            c_io = lax.broadcasted_iota(jnp.int32, (rows, tk), 1)
            if maps == 2:
                r_io = jnp.where(r_io >= tq, r_io - tq, r_io)
            vis = ((kj * tk + c_io) >> 6) <= ((first_row + r_io) >> 6)
        for h in range(N_HEADS):
            sl = slice(h * LANES, (h + 1) * LANES)
            q = q_ref[:, sl]
            if maps == 2:
                lane = lax.broadcasted_iota(jnp.int32, (tq, LANES), 1)
                zero = jnp.zeros_like(q)
                q = jnp.concatenate([jnp.where(lane < D_B, q, zero), jnp.where(lane >= D_B, q, zero)], axis=0)
            s = lax.dot_general(q, k_ref[:, sl], (((1,), (1,)), ((), ())), preferred_element_type=F32)
            if masked:
                s = jnp.where(vis, s, -jnp.inf)
            m_prev = m_sc[h]
            m_next = jnp.maximum(m_prev, jnp.max(s, axis=1, keepdims=True))
            alpha = jnp.exp(m_prev - m_next)
            p = jnp.exp(s - m_next[:, 0:1])
            l_sc[h] = alpha * l_sc[h] + jnp.sum(p, axis=1, keepdims=True)
            acc_sc[h] = alpha * acc_sc[h] + _dot(p.astype(BF16), v_ref[:, sl])
            m_sc[h] = m_next

    @pl.when(jnp.logical_and(active, jnp.logical_not(needs_mask)))
    def _plain():
        body(False)

    @pl.when(jnp.logical_and(active, needs_mask))
    def _masked():
        body(True)

    @pl.when(kj == nk - 1)
    def _finish():
        if maps == 2:
            lv = lamv_ref[...]
            lam = (jnp.exp(jnp.sum(lv[0:1] * lv[1:2], axis=-1, keepdims=True))
                   - jnp.exp(jnp.sum(lv[2:3] * lv[3:4], axis=-1, keepdims=True)) + lam_init)
        for h in range(N_HEADS):
            sl = slice(h * LANES, (h + 1) * LANES)
            o = acc_sc[h] / l_sc[h]
            if maps == 2:
                o = o[:tq] - lam * o[tq:]
                o = _row_rms(o, gsub_ref[...]) * (1.0 - lam_init)
            o_ref[:, sl] = o.astype(o_ref.dtype)


def _attention(q, k, v, lamv, gsub, *, maps, q_off, lam_init, tq, tk):
    b, sq, _ = q.shape
    sk = k.shape[1]
    assert sq % tq == 0 and sk % tk == 0
    grid = (b, sq // tq, sk // tk)

    def kv_map(bi, qi, kj):
        return (bi, jnp.minimum(kj, _num_kv_blocks(qi, tq, tk, q_off, sk) - 1), 0)

    rows = maps * tq
    kern = functools.partial(_attn_kernel, maps=maps, tq=tq, tk=tk, q_off=q_off, sk=sk, lam_init=lam_init)
    return pl.pallas_call(
        kern, grid=grid,
        in_specs=[_const_spec((4, LANES)), _const_spec((1, LANES)),
                  pl.BlockSpec((None, tq, D_MODEL), lambda bi, qi, kj: (bi, qi, 0)),
                  pl.BlockSpec((None, tk, D_MODEL), kv_map),
                  pl.BlockSpec((None, tk, D_MODEL), kv_map)],
        out_specs=pl.BlockSpec((None, tq, D_MODEL), lambda bi, qi, kj: (bi, qi, 0)),
        out_shape=jax.ShapeDtypeStruct((b, sq, D_MODEL), BF16),
        scratch_shapes=[pltpu.VMEM((N_HEADS, rows, LANES), F32)] * 3,
        compiler_params=pltpu.CompilerParams(dimension_semantics=("parallel", "parallel", "arbitrary"),
                                             vmem_limit_bytes=VMEM_LIMIT),
        name="attn_diff" if maps == 2 else "attn_mla",
    )(lamv, gsub, q, k, v)


def _merge_kernel(x_ref, oa_ref, ob_ref, ga_ref, gb_ref, wout_ref, gffn_ref, wrh_ref, wrl_ref, br_ref,
                  h_ref, hn_ref, route_ref):
    merged = (ga_ref[...].astype(F32) * oa_ref[...].astype(F32)
              + gb_ref[...].astype(F32) * ob_ref[...].astype(F32))
    h = x_ref[...] + _dot(merged.astype(BF16), wout_ref[...])
    h_ref[...] = h
    hn = _row_rms(h, gffn_ref[...])
    hn_ref[...] = hn

    hi = hn.astype(BF16)
    lo = (hn - hi.astype(F32)).astype(BF16)
    logits = _dot(hi, wrh_ref[...]) + (_dot(lo, wrh_ref[...]) + _dot(hi, wrl_ref[...])) + br_ref[...]
    lane = lax.broadcasted_iota(jnp.int32, logits.shape, 1).astype(F32)
    neg = jnp.full_like(logits, -jnp.inf)
    big = jnp.full_like(logits, 1e9)

    def first_argmax(vals):
        vmax = jnp.max(vals, axis=1, keepdims=True)
        return vmax, jnp.min(jnp.where(vals == vmax, lane, big), axis=1, keepdims=True)

    gmask = lane < N_GROUPS
    gmax, gsel = first_argmax(jnp.where(gmask, logits, neg))
    p_top = 1.0 / jnp.sum(jnp.where(gmask, jnp.exp(logits - gmax), 0.0), axis=1, keepdims=True)
    e_lo = N_GROUPS + gsel * EXP_PER_GROUP
    emask = jnp.logical_and(lane >= e_lo, lane < e_lo + EXP_PER_GROUP)
    le = jnp.where(emask, logits, neg)
    v1, i1 = first_argmax(le)
    v2, i2 = first_argmax(jnp.where(lane == i1, neg, le))
    t = jnp.exp(v2 - v1)
    w1 = p_top / (1.0 + t)
    w2 = p_top * t / (1.0 + t)
    route = jnp.where(lane == 0, i1 - N_GROUPS, 0.0)
    route = jnp.where(lane == 1, i2 - N_GROUPS, route)
    route = jnp.where(lane == 2, w1, route)
    route = jnp.where(lane == 3, w2, route)
    route_ref[...] = route


def _merge(x, oa, ob, ga, gb, wp):
    n = x.shape[0]
    tm = 256
    assert n % tm == 0

    def tok(width):
        return pl.BlockSpec((tm, width), lambda i: (i, 0))

    return pl.pallas_call(
        _merge_kernel, grid=(n // tm,),
        in_specs=[tok(D_MODEL)] * 5 + [_const_spec((D_MODEL, D_MODEL)), _const_spec((1, D_MODEL)),
                                       _const_spec((D_MODEL, LANES)), _const_spec((D_MODEL, LANES)),
                                       _const_spec((1, LANES))],
        out_specs=[tok(D_MODEL), tok(D_MODEL), tok(LANES)],
        out_shape=[jax.ShapeDtypeStruct((n, D_MODEL), F32), jax.ShapeDtypeStruct((n, D_MODEL), F32),
                   jax.ShapeDtypeStruct((n, LANES), F32)],
        compiler_params=pltpu.CompilerParams(dimension_semantics=("parallel",), vmem_limit_bytes=VMEM_LIMIT),
        name="merge",
    )(x, oa, ob, ga, gb, wp["w_out"], wp["g_ffn"], wp["w_r_hi"], wp["w_r_lo"], wp["b_r"])


MOE_TILE = 256


def _gather_rows(idx_ref, n_rows, src_hbm, dst_ref, sem):
    def body(r, carry):
        pltpu.make_async_copy(src_hbm.at[pl.ds(idx_ref[0, 0, r], 1), :], dst_ref.at[pl.ds(r, 1), :], sem).start()
        return carry
    lax.fori_loop(0, n_rows, body, 0, unroll=8)


def _wait_rows(n_rows, src_hbm, dst_ref, sem):
    pltpu.make_async_copy(src_hbm.at[pl.ds(0, n_rows), :], dst_ref, sem).wait()


def _moe_kernel(texp_ref, nused_ref, idx_ref, idxn_ref, hn_hbm, wei_ref, weo_ref, y_ref, xbuf, sem):
    del texp_ref
    i = pl.program_id(0)
    n_used = nused_ref[0]
    slot = i % 2

    @pl.when(jnp.logical_and(i == 0, n_used > 0))
    def _first():
        _gather_rows(idx_ref, MOE_TILE, hn_hbm, xbuf.at[0], sem.at[0])

    @pl.when(i + 1 < n_used)
    def _prefetch():
        _gather_rows(idxn_ref, MOE_TILE, hn_hbm, xbuf.at[1 - slot], sem.at[1 - slot])

    @pl.when(i < n_used)
    def _compute():
        _wait_rows(MOE_TILE, hn_hbm, xbuf.at[slot], sem.at[slot])
        x = xbuf[slot].astype(BF16)
        ab = _dot(x, wei_ref[...])
        a = ab[:, :D_EXPERT]
        act = (a * _sigmoid(a)) * ab[:, D_EXPERT:]
        y_ref[...] = _dot(act.astype(BF16), weo_ref[...])

    @pl.when(i >= n_used)
    def _idle():
        y_ref[...] = jnp.zeros(y_ref.shape, F32)


def _moe(hn, tile_expert, n_used, src_tok, wp):
    n_tiles = tile_expert.shape[0]
    idx3 = src_tok.reshape(n_tiles, 1, MOE_TILE)
    idx_spec = pl.BlockSpec((1, 1, MOE_TILE), lambda i, te, nu: (i, 0, 0), memory_space=pltpu.SMEM)
    idxn_spec = pl.BlockSpec((1, 1, MOE_TILE), lambda i, te, nu: (jnp.minimum(i + 1, n_tiles - 1), 0, 0),
                             memory_space=pltpu.SMEM)
    grid_spec = pltpu.PrefetchScalarGridSpec(
        num_scalar_prefetch=2, grid=(n_tiles,),
        in_specs=[idx_spec, idxn_spec, pl.BlockSpec(memory_space=pl.ANY),
                  pl.BlockSpec((None, D_MODEL, 2 * D_EXPERT), lambda i, te, nu: (te[i], 0, 0)),
                  pl.BlockSpec((None, D_EXPERT, D_MODEL), lambda i, te, nu: (te[i], 0, 0))],
        out_specs=pl.BlockSpec((MOE_TILE, D_MODEL), lambda i, te, nu: (i, 0)),
        scratch_shapes=[pltpu.VMEM((2, MOE_TILE, D_MODEL), F32), pltpu.SemaphoreType.DMA((2,))],
    )
    return pl.pallas_call(
        _moe_kernel, grid_spec=grid_spec,
        out_shape=jax.ShapeDtypeStruct((n_tiles * MOE_TILE, D_MODEL), F32),
        compiler_params=pltpu.CompilerParams(dimension_semantics=("arbitrary",), vmem_limit_bytes=VMEM_LIMIT),
        name="moe",
    )(tile_expert, n_used, idx3, idx3, hn, wp["w_exp_in"], wp["w_exp_out"])


def _routing_tables(route, n_tiles):
    n = route.shape[0]
    e = route[:, 0:2].astype(jnp.int32).reshape(-1)
    onehot = (e[:, None] == jnp.arange(N_EXPERTS, dtype=jnp.int32)[None, :]).astype(jnp.int32)
    rank = jnp.sum((jnp.cumsum(onehot, axis=0) - onehot) * onehot, axis=1)
    counts = jnp.sum(onehot, axis=0)
    tiles_per = (counts + MOE_TILE - 1) // MOE_TILE
    tile_end = jnp.cumsum(tiles_per)
    start = (tile_end - tiles_per) * MOE_TILE
    pos = start[e] + rank
    src_tok = jnp.zeros((n_tiles * MOE_TILE,), jnp.int32).at[pos].set(jnp.arange(2 * n, dtype=jnp.int32) // 2)
    n_used = tile_end[-1]
    tile_ids = jnp.minimum(jnp.arange(n_tiles, dtype=jnp.int32), n_used - 1)
    tile_expert = jnp.sum((tile_ids[:, None] >= tile_end[None, :]).astype(jnp.int32), axis=1)
    return tile_expert.astype(jnp.int32), n_used.reshape(1).astype(jnp.int32), src_tok, pos.reshape(n, 2)


COMB_TILE = 256


def _combine_kernel(pos_ref, posn_ref, route_ref, h_ref, pe_ref, y_hbm, gple_ref, wg_ref, wp_ref,
                    o_ref, ybuf, sem):
    i = pl.program_id(0)
    n = pl.num_programs(0)
    slot = i % 2
    rows = 2 * COMB_TILE

    @pl.when(i == 0)
    def _first():
        _gather_rows(pos_ref, rows, y_hbm, ybuf.at[0], sem.at[0])

    @pl.when(i + 1 < n)
    def _prefetch():
        _gather_rows(posn_ref, rows, y_hbm, ybuf.at[1 - slot], sem.at[1 - slot])

    _wait_rows(rows, y_hbm, ybuf.at[slot], sem.at[slot])
    route = route_ref[...]
    h = h_ref[...] + (route[:, 2:3] * ybuf[slot, 0:COMB_TILE, :] + route[:, 3:4] * ybuf[slot, COMB_TILE:rows, :])
    gate = _sigmoid(_dot(_row_rms(h, gple_ref[...]).astype(BF16), wg_ref[...]))
    o_ref[...] = h + gate * _dot(pe_ref[...].astype(BF16), wp_ref[...])


def _combine(route, h, pe, y, pos, wp):
    n = h.shape[0]
    assert n % COMB_TILE == 0
    n_tiles = n // COMB_TILE
    pos3 = pos.reshape(n_tiles, COMB_TILE, 2).transpose(0, 2, 1).reshape(n_tiles, 1, 2 * COMB_TILE)

    def tok(width):
        return pl.BlockSpec((COMB_TILE, width), lambda i: (i, 0))

    pos_spec = pl.BlockSpec((1, 1, 2 * COMB_TILE), lambda i: (i, 0, 0), memory_space=pltpu.SMEM)
    posn_spec = pl.BlockSpec((1, 1, 2 * COMB_TILE), lambda i: (jnp.minimum(i + 1, n_tiles - 1), 0, 0),
                             memory_space=pltpu.SMEM)
    return pl.pallas_call(
        _combine_kernel, grid=(n_tiles,),
        in_specs=[pos_spec, posn_spec, tok(LANES), tok(D_MODEL), tok(D_PLE), pl.BlockSpec(memory_space=pl.ANY),
                  _const_spec((1, D_MODEL)), _const_spec((D_MODEL, D_MODEL)), _const_spec((D_PLE, D_MODEL))],
        out_specs=tok(D_MODEL),
        out_shape=jax.ShapeDtypeStruct((n, D_MODEL), F32),
        scratch_shapes=[pltpu.VMEM((2, 2 * COMB_TILE, D_MODEL), F32), pltpu.SemaphoreType.DMA((2,))],
        compiler_params=pltpu.CompilerParams(dimension_semantics=("arbitrary",), vmem_limit_bytes=VMEM_LIMIT),
        name="combine",
    )(pos3, pos3, route, h, pe, y, wp["g_ple"], wp["w_ple_gate"], wp["w_ple_proj"])


def _rope_tables(pos, rot, offsets):
    half = rot // 2
    inv = ROPE_THETA ** (-jnp.arange(half, dtype=F32) * (2.0 / rot))
    ang = pos.astype(F32)[:, None] * inv[None, :]
    c, s = jnp.cos(ang), jnp.sin(ang)
    n = pos.shape[0]
    tc = jnp.ones((n, LANES), F32)
    t1 = jnp.zeros((n, LANES), F32)
    t2 = jnp.zeros((n, LANES), F32)
    for o in offsets:
        tc = tc.at[:, o:o + half].set(c).at[:, o + half:o + rot].set(c)
        t1 = t1.at[:, o:o + half].set(-s)
        t2 = t2.at[:, o + half:o + rot].set(s)
    return jnp.stack([tc, t1, t2])


def _prep_weights(l, g_mix, w_in, g_q_lat, w_uq, g_kv_lat, w_uk, w_uv, g_mla_qn, g_mla_qr, g_mla_kn, g_mla_kr,
                  g_diff_q, g_diff_k, lambda_q1, lambda_k1, lambda_q2, lambda_k2, g_diff_sub, w_out,
                  g_ffn, w_router_grp, b_router_grp, w_router_exp, b_router_exp, w_exp_in, w_exp_out,
                  g_ple, w_ple_gate, w_ple_proj):
    offs = [0]
    for sz in IN_SIZES:
        offs.append(offs[-1] + sz)
    wi = w_in[l]
    seg = [wi[:, offs[j]:offs[j + 1]] for j in range(len(IN_SIZES))]
    w_kr_pad = jnp.pad(seg[2], ((0, 0), (0, LANES - QK_ROPE)))
    w_main = jnp.concatenate([seg[0], seg[1], w_kr_pad] + seg[3:], axis=1).astype(BF16)

    hd = QK_NOPE + QK_ROPE
    uq = w_uq[l].reshape(Q_LORA, H_A, hd)
    uq_pad = jnp.concatenate([uq[:, :, QK_NOPE:], jnp.zeros((Q_LORA, H_A, LANES - hd), F32), uq[:, :, :QK_NOPE]],
                             axis=2).reshape(Q_LORA, H_A * LANES).astype(BF16)
    uk = w_uk[l].reshape(KV_LORA, H_A, QK_NOPE)
    uk_pad = jnp.concatenate([jnp.zeros((KV_LORA, H_A, LANES - QK_NOPE), F32), uk], axis=2)
    uk_pad = uk_pad.reshape(KV_LORA, H_A * LANES).astype(BF16)

    z32 = jnp.zeros((LANES - hd,), F32)
    gvecs = jnp.stack([
        jnp.concatenate([g_mla_qr[l], z32, g_mla_qn[l]]),
        jnp.concatenate([g_mla_kr[l], jnp.zeros((LANES - QK_ROPE,), F32)]),
        jnp.concatenate([g_diff_q[l], g_diff_q[l]]),
        jnp.concatenate([g_diff_k[l], g_diff_k[l]]),
        jnp.concatenate([jnp.zeros((LANES - QK_NOPE,), F32), g_mla_kn[l]]),
        jnp.zeros((LANES,), F32), jnp.zeros((LANES,), F32), jnp.zeros((LANES,), F32)])

    lane = jnp.arange(LANES)
    seg_a = jnp.where(lane < QK_ROPE, 0, jnp.where(lane < LANES - QK_NOPE, -1, 1))
    same_a = (seg_a[:, None] == seg_a[None, :]) & (seg_a[:, None] >= 0)
    bd_a = jnp.where(same_a, jnp.where(seg_a[:, None] == 0, 1.0 / QK_ROPE, 1.0 / QK_NOPE), 0.0)
    seg_b = lane // D_B
    bd_b = jnp.where(seg_b[:, None] == seg_b[None, :], 1.0 / D_B, 0.0)
    bd = jnp.stack([bd_a, bd_b]).astype(BF16)

    w_r = jnp.concatenate([w_router_grp[l], w_router_exp[l],
                           jnp.zeros((D_MODEL, LANES - N_GROUPS - N_EXPERTS), F32)], axis=1)
    w_r_hi = w_r.astype(BF16)
    w_r_lo = (w_r - w_r_hi.astype(F32)).astype(BF16)
    b_r = jnp.concatenate([b_router_grp[l], b_router_exp[l],
                           jnp.zeros((LANES - N_GROUPS - N_EXPERTS,), F32)]).reshape(1, LANES)

    pad64 = jnp.zeros((LANES - D_B,), F32)
    lamv = jnp.stack([jnp.concatenate([v[l], pad64]) for v in (lambda_q1, lambda_k1, lambda_q2, lambda_k2)])

    return {
        "g_mix": g_mix[l].reshape(1, D_MODEL), "w_main": w_main,
        "g_q_lat": g_q_lat[l].reshape(1, Q_LORA), "w_uq": uq_pad,
        "g_kv_lat": g_kv_lat[l].reshape(1, KV_LORA), "w_uk": uk_pad, "w_uv": w_uv[l].astype(BF16),
        "gvecs": gvecs, "bd": bd, "lamv": lamv, "g_diff_sub": g_diff_sub[l].reshape(1, V_B),
        "w_out": w_out[l].astype(BF16), "g_ffn": g_ffn[l].reshape(1, D_MODEL),
        "w_r_hi": w_r_hi, "w_r_lo": w_r_lo, "b_r": b_r,
        "w_exp_in": w_exp_in[l].astype(BF16), "w_exp_out": w_exp_out[l].astype(BF16),
        "g_ple": g_ple[l].reshape(1, D_MODEL), "w_ple_gate": w_ple_gate[l].astype(BF16),
        "w_ple_proj": w_ple_proj[l].astype(BF16),
    }


def _layer(x, pe, pos, past, wp, lam_init):
    b, s, _ = x.shape
    tab_a = _rope_tables(pos, QK_ROPE, (0,))
    tab_b = _rope_tables(pos, ROT_B, (0, D_B))
    q_cat, ckv, kr, dq, dk32, dk16, dv32, dv16, ga, gb = _proj(x, tab_a, tab_b, wp)

    if past is None:
        ckv_all, kr_all, dk_all, dv_all = ckv, kr, dk16, dv16
        q_off = 0
        tq = tk = min(256, s)
    else:
        ckv_p, kr_p, dk_p, dv_p = past
        past_len = ckv_p.shape[1]
        ckv_all = jnp.concatenate([ckv_p, ckv], axis=1)
        kr_all = jnp.concatenate([jnp.pad(kr_p, ((0, 0), (0, 0), (0, LANES - QK_ROPE))), kr], axis=1)
        dk_all = jnp.concatenate([dk_p.reshape(b, past_len, D_MODEL).astype(BF16), dk16], axis=1)
        dv_all = jnp.concatenate([dv_p.reshape(b, past_len, D_MODEL).astype(BF16), dv16], axis=1)
        q_off = past_len
        tq, tk = s, past_len + s
    sk = ckv_all.shape[1]

    k_cat, v_a = _kv_up(ckv_all.reshape(b * sk, KV_LORA), kr_all.reshape(b * sk, LANES), wp)
    o_a = _attention(q_cat, k_cat.reshape(b, sk, D_MODEL), v_a.reshape(b, sk, D_MODEL), wp["lamv"], wp["g_diff_sub"],
                     maps=1, q_off=q_off, lam_init=lam_init, tq=tq, tk=tk)
    o_b = _attention(dq, dk_all, dv_all, wp["lamv"], wp["g_diff_sub"],
                     maps=2, q_off=q_off, lam_init=lam_init, tq=tq, tk=tk)

    n = b * s
    flat = lambda a: a.reshape(n, a.shape[-1])
    h, hn, route = _merge(flat(x), flat(o_a), flat(o_b), flat(ga), flat(gb), wp)

    n_tiles = (2 * n) // MOE_TILE + N_EXPERTS
    tile_expert, n_used, src_tok, slot_of = _routing_tables(route, n_tiles)
    y = _moe(hn, tile_expert, n_used, src_tok, wp)
    out = _combine(route, h, flat(pe), y, slot_of, wp)

    return (out.reshape(b, s, D_MODEL),
            (ckv, kr[:, :, :QK_ROPE], dk32.reshape(b, s, H_B, 2, D_B), dv32.reshape(b, s, H_B, V_B)))


def kernel(x_prompt, x_sample, p_prompt, p_sample, cache_mla_ckv, cache_mla_krope, cache_diff_k, cache_diff_v,
           g_mix, w_in, g_q_lat, w_uq, g_kv_lat, w_uk, w_uv, g_mla_qn, g_mla_qr, g_mla_kn, g_mla_kr,
           g_diff_q, g_diff_k, lambda_q1, lambda_k1, lambda_q2, lambda_k2, g_diff_sub, w_out,
           g_ffn, w_router_grp, b_router_grp, w_router_exp, b_router_exp, w_exp_in, w_exp_out,
           g_ple, w_ple_gate, w_ple_proj):
    depth = w_in.shape[0]
    pos_p = jnp.arange(x_prompt.shape[1], dtype=jnp.int32)
    pos_s = cache_mla_ckv.shape[2] + jnp.arange(x_sample.shape[1], dtype=jnp.int32)
    hp, hs = x_prompt, x_sample
    st_p, st_s = [], []
    for l in range(depth):
        wp = _prep_weights(l, g_mix, w_in, g_q_lat, w_uq, g_kv_lat, w_uk, w_uv, g_mla_qn, g_mla_qr, g_mla_kn,
                           g_mla_kr, g_diff_q, g_diff_k, lambda_q1, lambda_k1, lambda_q2, lambda_k2, g_diff_sub,
                           w_out, g_ffn, w_router_grp, b_router_grp, w_router_exp, b_router_exp, w_exp_in,
                           w_exp_out, g_ple, w_ple_gate, w_ple_proj)
        lam_init = 0.8 - 0.6 * math.exp(-0.3 * l)
        hp, sp = _layer(hp, p_prompt[l], pos_p, None, wp, lam_init)
        hs, ss = _layer(hs, p_sample[l], pos_s,
                        (cache_mla_ckv[l], cache_mla_krope[l], cache_diff_k[l], cache_diff_v[l]), wp, lam_init)
        st_p.append(sp)
        st_s.append(ss)
    stack = lambda sts, j: jnp.stack([st[j] for st in sts])
    return (hp, hs,
            stack(st_p, 0), stack(st_p, 1), stack(st_p, 2), stack(st_p, 3),
            stack(st_s, 0), stack(st_s, 1), stack(st_s, 2), stack(st_s, 3))
```

```python
import functools
import math

import jax
import jax.numpy as jnp
from jax import lax
from jax.experimental import pallas as pl
from jax.experimental.pallas import tpu as pltpu

F32 = jnp.float32
BF16 = jnp.bfloat16

D_MODEL = 1024
CHUNK = 64
ROPE_THETA = 500000.0
EPS = 1e-6
H_A = 8
Q_LORA = 384
KV_LORA = 256
QK_NOPE = 64
QK_ROPE = 32
V_A = 128
MLA_SCALE = 1.0 / math.sqrt(QK_NOPE + QK_ROPE)
H_B = 8
D_B = 64
V_B = 2 * D_B
ROT_B = D_B // 4
DIFF_SCALE = 1.0 / math.sqrt(D_B)
LOG2E = math.log2(math.e)
N_GROUPS = 4
EXP_PER_GROUP = 8
N_EXPERTS = N_GROUPS * EXP_PER_GROUP
D_EXPERT = 256
D_PLE = 256
IN_SIZES = (Q_LORA, KV_LORA, QK_ROPE, H_B * 2 * D_B, H_B * 2 * D_B, H_B * V_B, D_MODEL, D_MODEL)

LANES = 128
N_HEADS = 8
VMEM_LIMIT = 56 * 1024 * 1024

_O_Q = 0
_O_CKV = _O_Q + Q_LORA
_O_KR = _O_CKV + KV_LORA
_O_DQ = _O_KR + LANES
_O_DK = _O_DQ + D_MODEL
_O_DV = _O_DK + D_MODEL
_O_GA = _O_DV + D_MODEL
_O_GB = _O_GA + D_MODEL
_W_MAIN_COLS = _O_GB + D_MODEL


def _dot(a, b):
    return jnp.dot(a, b, preferred_element_type=F32)


def _sigmoid(x):
    return 1.0 / (1.0 + jnp.exp(-x))


def _row_rms(x, g):
    ms = jnp.mean(x * x, axis=-1, keepdims=True)
    return (x * lax.rsqrt(ms + EPS)) * g


def _seg_rms(z, bd, g):
    ms = _dot((z * z).astype(BF16), bd)
    return (z * lax.rsqrt(ms + EPS)) * g


def _rope_block(y, tab_ref, half):
    return (y * tab_ref[0]
            + pltpu.roll(y, LANES - half, 1) * tab_ref[1]
            + pltpu.roll(y, half, 1) * tab_ref[2])


def _proj_kernel(x_ref, taba_ref, tabb_ref, gmix_ref, w_ref, gql_ref, wuq_ref, gkv_ref, gv_ref, bd_ref,
                 qcat_ref, ckv_ref, kr_ref, dq_ref, dk32_ref, dk16_ref, dv32_ref, dv16_ref, ga_ref, gb_ref):
    xn = _row_rms(x_ref[...], gmix_ref[...]).astype(BF16)
    bd_a = bd_ref[0]
    bd_b = bd_ref[1]

    ql = _row_rms(_dot(xn, w_ref[:, _O_Q:_O_Q + Q_LORA]), gql_ref[...]).astype(BF16)
    q = _dot(ql, wuq_ref[...])
    for h in range(N_HEADS):
        sl = slice(h * LANES, (h + 1) * LANES)
        qb = _rope_block(_seg_rms(q[:, sl], bd_a, gv_ref[0:1, :]), taba_ref, QK_ROPE // 2)
        qcat_ref[:, sl] = (qb * (MLA_SCALE * LOG2E)).astype(BF16)

    ckv_ref[...] = _row_rms(_dot(xn, w_ref[:, _O_CKV:_O_CKV + KV_LORA]), gkv_ref[...])
    kr = _dot(xn, w_ref[:, _O_KR:_O_KR + LANES])
    kr_ref[...] = _rope_block(_seg_rms(kr, bd_a, gv_ref[1:2, :]), taba_ref, QK_ROPE // 2)

    zq = _dot(xn, w_ref[:, _O_DQ:_O_DQ + D_MODEL])
    zk = _dot(xn, w_ref[:, _O_DK:_O_DK + D_MODEL])
    for h in range(N_HEADS):
        sl = slice(h * LANES, (h + 1) * LANES)
        qb = _rope_block(_seg_rms(zq[:, sl], bd_b, gv_ref[2:3, :]), tabb_ref, ROT_B // 2)
        dq_ref[:, sl] = (qb * (DIFF_SCALE * LOG2E)).astype(BF16)
        kb = _rope_block(_seg_rms(zk[:, sl], bd_b, gv_ref[3:4, :]), tabb_ref, ROT_B // 2)
        dk32_ref[:, sl] = kb
        dk16_ref[:, sl] = kb.astype(BF16)

    dv = _dot(xn, w_ref[:, _O_DV:_O_DV + D_MODEL])
    dv32_ref[...] = dv
    dv16_ref[...] = dv.astype(BF16)
    ga_ref[...] = _sigmoid(_dot(xn, w_ref[:, _O_GA:_O_GA + D_MODEL])).astype(BF16)
    gb_ref[...] = _sigmoid(_dot(xn, w_ref[:, _O_GB:_O_GB + D_MODEL])).astype(BF16)


def _const_spec(shape):
    nd = len(shape)
    return pl.BlockSpec(shape, lambda *_: (0,) * nd, pipeline_mode=pl.Buffered(1))


def _proj(x, tab_a, tab_b, wp):
    b, s, _ = x.shape
    tm = min(256, s)
    grid = (b, s // tm)

    def tok(width):
        return pl.BlockSpec((None, tm, width), lambda bi, i: (bi, i, 0))

    tab_spec = pl.BlockSpec((3, tm, LANES), lambda bi, i: (0, i, 0))
    out_shapes = [
        jax.ShapeDtypeStruct((b, s, D_MODEL), BF16),
        jax.ShapeDtypeStruct((b, s, KV_LORA), F32),
        jax.ShapeDtypeStruct((b, s, LANES), F32),
        jax.ShapeDtypeStruct((b, s, D_MODEL), BF16),
        jax.ShapeDtypeStruct((b, s, D_MODEL), F32),
        jax.ShapeDtypeStruct((b, s, D_MODEL), BF16),
        jax.ShapeDtypeStruct((b, s, D_MODEL), F32),
        jax.ShapeDtypeStruct((b, s, D_MODEL), BF16),
        jax.ShapeDtypeStruct((b, s, D_MODEL), BF16),
        jax.ShapeDtypeStruct((b, s, D_MODEL), BF16),
    ]
    out_specs = [tok(D_MODEL), tok(KV_LORA), tok(LANES), tok(D_MODEL), tok(D_MODEL), tok(D_MODEL),
                 tok(D_MODEL), tok(D_MODEL), tok(D_MODEL), tok(D_MODEL)]
    in_specs = [
        tok(D_MODEL), tab_spec, tab_spec,
        _const_spec((1, D_MODEL)), _const_spec((D_MODEL, _W_MAIN_COLS)),
        _const_spec((1, Q_LORA)), _const_spec((Q_LORA, D_MODEL)), _const_spec((1, KV_LORA)),
        _const_spec((8, LANES)), _const_spec((2, LANES, LANES)),
    ]
    return pl.pallas_call(
        _proj_kernel, grid=grid, in_specs=in_specs, out_specs=out_specs, out_shape=out_shapes,
        compiler_params=pltpu.CompilerParams(dimension_semantics=("parallel", "parallel"),
                                             vmem_limit_bytes=VMEM_LIMIT),
        name="proj",
    )(x, tab_a, tab_b, wp["g_mix"], wp["w_main"], wp["g_q_lat"], wp["w_uq"], wp["g_kv_lat"],
      wp["gvecs"], wp["bd"])


def _kvup_kernel(ckv_ref, kr_ref, wuk_ref, wuv_ref, gv_ref, bd_ref, kcat_ref, va_ref):
    c = ckv_ref[...].astype(BF16)
    kn = _dot(c, wuk_ref[...])
    kr = kr_ref[...]
    bd_a = bd_ref[0]
    for h in range(N_HEADS):
        sl = slice(h * LANES, (h + 1) * LANES)
        kcat_ref[:, sl] = (_seg_rms(kn[:, sl], bd_a, gv_ref[4:5, :]) + kr).astype(BF16)
    va_ref[...] = _dot(c, wuv_ref[...]).astype(BF16)


def _kv_up(ckv, kr, wp):
    m = ckv.shape[0]
    tm = 256
    assert m % tm == 0

    def tok(width):
        return pl.BlockSpec((tm, width), lambda i: (i, 0))

    return pl.pallas_call(
        _kvup_kernel, grid=(m // tm,),
        in_specs=[tok(KV_LORA), tok(LANES), _const_spec((KV_LORA, D_MODEL)), _const_spec((KV_LORA, D_MODEL)),
                  _const_spec((8, LANES)), _const_spec((2, LANES, LANES))],
        out_specs=[tok(D_MODEL), tok(D_MODEL)],
        out_shape=[jax.ShapeDtypeStruct((m, D_MODEL), BF16), jax.ShapeDtypeStruct((m, D_MODEL), BF16)],
        compiler_params=pltpu.CompilerParams(dimension_semantics=("parallel",), vmem_limit_bytes=VMEM_LIMIT),
        name="kv_up",
    )(ckv, kr, wp["w_uk"], wp["w_uv"], wp["gvecs"], wp["bd"])


def _attn_kernel(lamv_ref, gsub_ref, q_ref, k_ref, v_ref, o_ref, qt_sc, m_sc, l_sc, acc_sc, st0_sc, st1_sc,
                 *, maps, tq, q_off, lam_init):
    qi = pl.program_id(1)
    rows = maps * tq
    n_full = q_off // tq + qi

    eye = (lax.broadcasted_iota(jnp.int32, (LANES, LANES), 0)
           == lax.broadcasted_iota(jnp.int32, (LANES, LANES), 1)).astype(BF16)
    for h in range(N_HEADS):
        sl = slice(h * LANES, (h + 1) * LANES)
        qt = lax.dot_general(eye, q_ref[:, sl], (((1,), (1,)), ((), ())), preferred_element_type=F32).astype(BF16)
        if maps == 2:
            sub = lax.broadcasted_iota(jnp.int32, (LANES, tq), 0)
            zero = jnp.zeros_like(qt)
            qt = jnp.concatenate([jnp.where(sub < D_B, qt, zero), jnp.where(sub >= D_B, qt, zero)], axis=1)
        qt_sc[h] = qt
        m_sc[h] = jnp.full((1, rows), -jnp.inf, F32)
        l_sc[h] = jnp.zeros((1, rows), F32)
        acc_sc[h] = jnp.zeros((LANES, rows), F32)

    def scores(h, koff, buf):
        buf[h] = _dot(k_ref[pl.ds(koff, tq), h * LANES:(h + 1) * LANES], qt_sc[h])

    def softmax_pv(h, koff, buf, vis):
        sl = slice(h * LANES, (h + 1) * LANES)
        st = buf[h]
        if vis is not None:
            st = jnp.where(vis, st, -jnp.inf)
        m_prev = m_sc[h]
        m_next = jnp.maximum(m_prev, jnp.max(st, axis=0, keepdims=True))
        alpha = jnp.exp2(m_prev - m_next)
        pt = jnp.exp2(st - m_next)
        l_sc[h] = alpha * l_sc[h] + jnp.sum(pt, axis=0, keepdims=True)
        pv = lax.dot_general(v_ref[pl.ds(koff, tq), sl], pt.astype(BF16), (((0,), (0,)), ((), ())),
                             preferred_element_type=F32)
        acc_sc[h] = alpha * acc_sc[h] + pv
        m_sc[h] = m_next

    for h in range(N_HEADS):
        scores(h, 0, st0_sc)

    def step(c, cur, nxt):
        koff = pl.multiple_of(c * tq, tq)
        knext = pl.multiple_of((c + 1) * tq, tq)
        scores(0, knext, nxt)
        for h in range(N_HEADS):
            if h + 1 < N_HEADS:
                scores(h + 1, knext, nxt)
            softmax_pv(h, koff, cur, None)

    def chunk_pair(i, carry):
        step(2 * i, st0_sc, st1_sc)
        step(2 * i + 1, st1_sc, st0_sc)
        return carry

    lax.fori_loop(0, n_full // 2, chunk_pair, 0)
    odd = n_full % 2 == 1

    @pl.when(odd)
    def _odd_tail():
        step(n_full - 1, st0_sc, st1_sc)

    kc = lax.broadcasted_iota(jnp.int32, (tq, rows), 0) >> 6
    qc = lax.broadcasted_iota(jnp.int32, (tq, rows), 1)
    if maps == 2:
        qc = jnp.where(qc >= tq, qc - tq, qc)
    vis = kc <= (qc >> 6)
    kdiag = pl.multiple_of(n_full * tq, tq)

    @pl.when(odd)
    def _diag_odd():
        for h in range(N_HEADS):
            softmax_pv(h, kdiag, st1_sc, vis)

    @pl.when(jnp.logical_not(odd))
    def _diag_even():
        for h in range(N_HEADS):
            softmax_pv(h, kdiag, st0_sc, vis)

    if maps == 2:
        lv = lamv_ref[...]
        lam = (jnp.exp(jnp.sum(lv[0:1] * lv[1:2], axis=-1, keepdims=True))
               - jnp.exp(jnp.sum(lv[2:3] * lv[3:4], axis=-1, keepdims=True)) + lam_init)
    eye_q = (lax.broadcasted_iota(jnp.int32, (tq, tq), 0)
             == lax.broadcasted_iota(jnp.int32, (tq, tq), 1)).astype(BF16)
    for h in range(N_HEADS):
        sl = slice(h * LANES, (h + 1) * LANES)
        ot = acc_sc[h] / l_sc[h]
        if maps == 2:
            ot = ot[:, :tq] - lam * ot[:, tq:]
            ms = jnp.mean(ot * ot, axis=0, keepdims=True)
            ot = ((ot * lax.rsqrt(ms + EPS)) * gsub_ref[...]) * (1.0 - lam_init)
        o = lax.dot_general(eye_q, ot.astype(BF16), (((1,), (1,)), ((), ())), preferred_element_type=F32)
        o_ref[:, sl] = o.astype(o_ref.dtype)


def _attention(q, k, v, lamv, gsub, *, maps, q_off, lam_init, tq):
    b, sq, _ = q.shape
    sk = k.shape[1]
    assert sq % tq == 0 and q_off % tq == 0 and tq % CHUNK == 0 and sk == q_off + sq
    rows = maps * tq
    kern = functools.partial(_attn_kernel, maps=maps, tq=tq, q_off=q_off, lam_init=lam_init)
    kv_spec = pl.BlockSpec((None, sk, D_MODEL), lambda bi, qi: (bi, 0, 0), pipeline_mode=pl.Buffered(1))
    return pl.pallas_call(
        kern, grid=(b, sq // tq),
        in_specs=[_const_spec((4, LANES)), _const_spec((LANES, 1)),
                  pl.BlockSpec((None, tq, D_MODEL), lambda bi, qi: (bi, qi, 0)), kv_spec, kv_spec],
        out_specs=pl.BlockSpec((None, tq, D_MODEL), lambda bi, qi: (bi, qi, 0)),
        out_shape=jax.ShapeDtypeStruct((b, sq, D_MODEL), BF16),
        scratch_shapes=[pltpu.VMEM((N_HEADS, LANES, rows), BF16), pltpu.VMEM((N_HEADS, 1, rows), F32),
                        pltpu.VMEM((N_HEADS, 1, rows), F32), pltpu.VMEM((N_HEADS, LANES, rows), F32),
                        pltpu.VMEM((N_HEADS, tq, rows), F32), pltpu.VMEM((N_HEADS, tq, rows), F32)],
        compiler_params=pltpu.CompilerParams(dimension_semantics=("parallel", "arbitrary"),
                                             vmem_limit_bytes=VMEM_LIMIT),
        name="attn_diff" if maps == 2 else "attn_mla",
    )(lamv, gsub, q, k, v)


def _merge_kernel(x_ref, oa_ref, ob_ref, ga_ref, gb_ref, wout_ref, gffn_ref, wrh_ref, wrl_ref, br_ref,
                  h_ref, hn_ref, route_ref):
    merged = (ga_ref[...].astype(F32) * oa_ref[...].astype(F32)
              + gb_ref[...].astype(F32) * ob_ref[...].astype(F32))
    h = x_ref[...] + _dot(merged.astype(BF16), wout_ref[...])
    h_ref[...] = h
    hn = _row_rms(h, gffn_ref[...])
    hn_ref[...] = hn

    hi = hn.astype(BF16)
    lo = (hn - hi.astype(F32)).astype(BF16)
    logits = _dot(hi, wrh_ref[...]) + (_dot(lo, wrh_ref[...]) + _dot(hi, wrl_ref[...])) + br_ref[...]
    lane = lax.broadcasted_iota(jnp.int32, logits.shape, 1).astype(F32)
    neg = jnp.full_like(logits, -jnp.inf)
    big = jnp.full_like(logits, 1e9)

    def first_argmax(vals):
        vmax = jnp.max(vals, axis=1, keepdims=True)
        return vmax, jnp.min(jnp.where(vals == vmax, lane, big), axis=1, keepdims=True)

    gmask = lane < N_GROUPS
    gmax, gsel = first_argmax(jnp.where(gmask, logits, neg))
    p_top = 1.0 / jnp.sum(jnp.where(gmask, jnp.exp(logits - gmax), 0.0), axis=1, keepdims=True)
    e_lo = N_GROUPS + gsel * EXP_PER_GROUP
    emask = jnp.logical_and(lane >= e_lo, lane < e_lo + EXP_PER_GROUP)
    le = jnp.where(emask, logits, neg)
    v1, i1 = first_argmax(le)
    v2, i2 = first_argmax(jnp.where(lane == i1, neg, le))
    t = jnp.exp(v2 - v1)
    w1 = p_top / (1.0 + t)
    w2 = p_top * t / (1.0 + t)
    route = jnp.where(lane == 0, i1 - N_GROUPS, 0.0)
    route = jnp.where(lane == 1, i2 - N_GROUPS, route)
    route = jnp.where(lane == 2, w1, route)
    route = jnp.where(lane == 3, w2, route)
    route_ref[...] = route


def _merge(x, oa, ob, ga, gb, wp):
    n = x.shape[0]
    tm = 256
    assert n % tm == 0

    def tok(width):
        return pl.BlockSpec((tm, width), lambda i: (i, 0))

    return pl.pallas_call(
        _merge_kernel, grid=(n // tm,),
        in_specs=[tok(D_MODEL)] * 5 + [_const_spec((D_MODEL, D_MODEL)), _const_spec((1, D_MODEL)),
                                       _const_spec((D_MODEL, LANES)), _const_spec((D_MODEL, LANES)),
                                       _const_spec((1, LANES))],
        out_specs=[tok(D_MODEL), tok(D_MODEL), tok(LANES)],
        out_shape=[jax.ShapeDtypeStruct((n, D_MODEL), F32), jax.ShapeDtypeStruct((n, D_MODEL), F32),
                   jax.ShapeDtypeStruct((n, LANES), F32)],
        compiler_params=pltpu.CompilerParams(dimension_semantics=("parallel",), vmem_limit_bytes=VMEM_LIMIT),
        name="merge",
    )(x, oa, ob, ga, gb, wp["w_out"], wp["g_ffn"], wp["w_r_hi"], wp["w_r_lo"], wp["b_r"])


MOE_TILE = 256


def _gather_rows(idx_ref, n_rows, src_hbm, dst_ref, sem):
    def body(r, carry):
        pltpu.make_async_copy(src_hbm.at[pl.ds(idx_ref[0, 0, r], 1), :], dst_ref.at[pl.ds(r, 1), :], sem).start()
        return carry
    lax.fori_loop(0, n_rows, body, 0, unroll=8)


def _wait_rows(n_rows, src_hbm, dst_ref, sem):
    pltpu.make_async_copy(src_hbm.at[pl.ds(0, n_rows), :], dst_ref, sem).wait()


def _moe_kernel(texp_ref, nused_ref, idx_ref, idxn_ref, hn_hbm, wei_ref, weo_ref, y_ref, xbuf, sem):
    del texp_ref
    i = pl.program_id(0)
    n_used = nused_ref[0]
    slot = i % 2

    @pl.when(jnp.logical_and(i == 0, n_used > 0))
    def _first():
        _gather_rows(idx_ref, MOE_TILE, hn_hbm, xbuf.at[0], sem.at[0])

    @pl.when(i + 1 < n_used)
    def _prefetch():
        _gather_rows(idxn_ref, MOE_TILE, hn_hbm, xbuf.at[1 - slot], sem.at[1 - slot])

    @pl.when(i < n_used)
    def _compute():
        _wait_rows(MOE_TILE, hn_hbm, xbuf.at[slot], sem.at[slot])
        x = xbuf[slot].astype(BF16)
        ab = _dot(x, wei_ref[...])
        a = ab[:, :D_EXPERT]
        act = (a * _sigmoid(a)) * ab[:, D_EXPERT:]
        y_ref[...] = _dot(act.astype(BF16), weo_ref[...])

    @pl.when(i >= n_used)
    def _idle():
        y_ref[...] = jnp.zeros(y_ref.shape, F32)


def _moe(hn, tile_expert, n_used, src_tok, wp):
    n_tiles = tile_expert.shape[0]
    idx3 = src_tok.reshape(n_tiles, 1, MOE_TILE)
    idx_spec = pl.BlockSpec((1, 1, MOE_TILE), lambda i, te, nu: (i, 0, 0), memory_space=pltpu.SMEM)
    idxn_spec = pl.BlockSpec((1, 1, MOE_TILE), lambda i, te, nu: (jnp.minimum(i + 1, n_tiles - 1), 0, 0),
                             memory_space=pltpu.SMEM)
    grid_spec = pltpu.PrefetchScalarGridSpec(
        num_scalar_prefetch=2, grid=(n_tiles,),
        in_specs=[idx_spec, idxn_spec, pl.BlockSpec(memory_space=pl.ANY),
                  pl.BlockSpec((None, D_MODEL, 2 * D_EXPERT), lambda i, te, nu: (te[i], 0, 0)),
                  pl.BlockSpec((None, D_EXPERT, D_MODEL), lambda i, te, nu: (te[i], 0, 0))],
        out_specs=pl.BlockSpec((MOE_TILE, D_MODEL), lambda i, te, nu: (i, 0)),
        scratch_shapes=[pltpu.VMEM((2, MOE_TILE, D_MODEL), F32), pltpu.SemaphoreType.DMA((2,))],
    )
    return pl.pallas_call(
        _moe_kernel, grid_spec=grid_spec,
        out_shape=jax.ShapeDtypeStruct((n_tiles * MOE_TILE, D_MODEL), F32),
        compiler_params=pltpu.CompilerParams(dimension_semantics=("arbitrary",), vmem_limit_bytes=VMEM_LIMIT),
        name="moe",
    )(tile_expert, n_used, idx3, idx3, hn, wp["w_exp_in"], wp["w_exp_out"])


def _routing_tables(route, n_tiles):
    n = route.shape[0]
    e = route[:, 0:2].astype(jnp.int32).reshape(-1)
    onehot = (e[:, None] == jnp.arange(N_EXPERTS, dtype=jnp.int32)[None, :]).astype(jnp.int32)
    rank = jnp.sum((jnp.cumsum(onehot, axis=0) - onehot) * onehot, axis=1)
    counts = jnp.sum(onehot, axis=0)
    tiles_per = (counts + MOE_TILE - 1) // MOE_TILE
    tile_end = jnp.cumsum(tiles_per)
    start = (tile_end - tiles_per) * MOE_TILE
    pos = start[e] + rank
    filler = jnp.arange(n_tiles * MOE_TILE, dtype=jnp.int32) % n
    src_tok = filler.at[pos].set(jnp.arange(2 * n, dtype=jnp.int32) // 2)
    n_used = tile_end[-1]
    tile_ids = jnp.minimum(jnp.arange(n_tiles, dtype=jnp.int32), n_used - 1)
    tile_expert = jnp.sum((tile_ids[:, None] >= tile_end[None, :]).astype(jnp.int32), axis=1)
    return tile_expert.astype(jnp.int32), n_used.reshape(1).astype(jnp.int32), src_tok, pos.reshape(n, 2)


COMB_TILE = 256


def _combine_kernel(pos_ref, posn_ref, route_ref, h_ref, pe_ref, y_hbm, gple_ref, wg_ref, wp_ref,
                    o_ref, ybuf, sem):
    i = pl.program_id(0)
    n = pl.num_programs(0)
    slot = i % 2
    rows = 2 * COMB_TILE

    @pl.when(i == 0)
    def _first():
        _gather_rows(pos_ref, rows, y_hbm, ybuf.at[0], sem.at[0])

    @pl.when(i + 1 < n)
    def _prefetch():
        _gather_rows(posn_ref, rows, y_hbm, ybuf.at[1 - slot], sem.at[1 - slot])

    _wait_rows(rows, y_hbm, ybuf.at[slot], sem.at[slot])
    route = route_ref[...]
    h = h_ref[...] + (route[:, 2:3] * ybuf[slot, 0:COMB_TILE, :] + route[:, 3:4] * ybuf[slot, COMB_TILE:rows, :])
    gate = _sigmoid(_dot(_row_rms(h, gple_ref[...]).astype(BF16), wg_ref[...]))
    o_ref[...] = h + gate * _dot(pe_ref[...].astype(BF16), wp_ref[...])


def _combine(route, h, pe, y, pos, wp):
    n = h.shape[0]
    assert n % COMB_TILE == 0
    n_tiles = n // COMB_TILE
    pos3 = pos.reshape(n_tiles, COMB_TILE, 2).transpose(0, 2, 1).reshape(n_tiles, 1, 2 * COMB_TILE)

    def tok(width):
        return pl.BlockSpec((COMB_TILE, width), lambda i: (i, 0))

    pos_spec = pl.BlockSpec((1, 1, 2 * COMB_TILE), lambda i: (i, 0, 0), memory_space=pltpu.SMEM)
    posn_spec = pl.BlockSpec((1, 1, 2 * COMB_TILE), lambda i: (jnp.minimum(i + 1, n_tiles - 1), 0, 0),
                             memory_space=pltpu.SMEM)
    return pl.pallas_call(
        _combine_kernel, grid=(n_tiles,),
        in_specs=[pos_spec, posn_spec, tok(LANES), tok(D_MODEL), tok(D_PLE), pl.BlockSpec(memory_space=pl.ANY),
                  _const_spec((1, D_MODEL)), _const_spec((D_MODEL, D_MODEL)), _const_spec((D_PLE, D_MODEL))],
        out_specs=tok(D_MODEL),
        out_shape=jax.ShapeDtypeStruct((n, D_MODEL), F32),
        scratch_shapes=[pltpu.VMEM((2, 2 * COMB_TILE, D_MODEL), F32), pltpu.SemaphoreType.DMA((2,))],
        compiler_params=pltpu.CompilerParams(dimension_semantics=("arbitrary",), vmem_limit_bytes=VMEM_LIMIT),
        name="combine",
    )(pos3, pos3, route, h, pe, y, wp["g_ple"], wp["w_ple_gate"], wp["w_ple_proj"])


def _rope_tables(pos, rot, offsets):
    half = rot // 2
    inv = ROPE_THETA ** (-jnp.arange(half, dtype=F32) * (2.0 / rot))
    ang = pos.astype(F32)[:, None] * inv[None, :]
    c, s = jnp.cos(ang), jnp.sin(ang)
    n = pos.shape[0]
    tc = jnp.ones((n, LANES), F32)
    t1 = jnp.zeros((n, LANES), F32)
    t2 = jnp.zeros((n, LANES), F32)
    for o in offsets:
        tc = tc.at[:, o:o + half].set(c).at[:, o + half:o + rot].set(c)
        t1 = t1.at[:, o:o + half].set(-s)
        t2 = t2.at[:, o + half:o + rot].set(s)
    return jnp.stack([tc, t1, t2])


def _prep_weights(l, g_mix, w_in, g_q_lat, w_uq, g_kv_lat, w_uk, w_uv, g_mla_qn, g_mla_qr, g_mla_kn, g_mla_kr,
                  g_diff_q, g_diff_k, lambda_q1, lambda_k1, lambda_q2, lambda_k2, g_diff_sub, w_out,
                  g_ffn, w_router_grp, b_router_grp, w_router_exp, b_router_exp, w_exp_in, w_exp_out,
                  g_ple, w_ple_gate, w_ple_proj):
    offs = [0]
    for sz in IN_SIZES:
        offs.append(offs[-1] + sz)
    wi = w_in[l]
    seg = [wi[:, offs[j]:offs[j + 1]] for j in range(len(IN_SIZES))]
    w_kr_pad = jnp.pad(seg[2], ((0, 0), (0, LANES - QK_ROPE)))
    w_main = jnp.concatenate([seg[0], seg[1], w_kr_pad] + seg[3:], axis=1).astype(BF16)

    hd = QK_NOPE + QK_ROPE
    uq = w_uq[l].reshape(Q_LORA, H_A, hd)
    uq_pad = jnp.concatenate([uq[:, :, QK_NOPE:], jnp.zeros((Q_LORA, H_A, LANES - hd), F32), uq[:, :, :QK_NOPE]],
                             axis=2).reshape(Q_LORA, H_A * LANES).astype(BF16)
    uk = w_uk[l].reshape(KV_LORA, H_A, QK_NOPE)
    uk_pad = jnp.concatenate([jnp.zeros((KV_LORA, H_A, LANES - QK_NOPE), F32), uk], axis=2)
    uk_pad = uk_pad.reshape(KV_LORA, H_A * LANES).astype(BF16)

    z32 = jnp.zeros((LANES - hd,), F32)
    gvecs = jnp.stack([
        jnp.concatenate([g_mla_qr[l], z32, g_mla_qn[l]]),
        jnp.concatenate([g_mla_kr[l], jnp.zeros((LANES - QK_ROPE,), F32)]),
        jnp.concatenate([g_diff_q[l], g_diff_q[l]]),
        jnp.concatenate([g_diff_k[l], g_diff_k[l]]),
        jnp.concatenate([jnp.zeros((LANES - QK_NOPE,), F32), g_mla_kn[l]]),
        jnp.zeros((LANES,), F32), jnp.zeros((LANES,), F32), jnp.zeros((LANES,), F32)])

    lane = jnp.arange(LANES)
    seg_a = jnp.where(lane < QK_ROPE, 0, jnp.where(lane < LANES - QK_NOPE, -1, 1))
    same_a = (seg_a[:, None] == seg_a[None, :]) & (seg_a[:, None] >= 0)
    bd_a = jnp.where(same_a, jnp.where(seg_a[:, None] == 0, 1.0 / QK_ROPE, 1.0 / QK_NOPE), 0.0)
    seg_b = lane // D_B
    bd_b = jnp.where(seg_b[:, None] == seg_b[None, :], 1.0 / D_B, 0.0)
    bd = jnp.stack([bd_a, bd_b]).astype(BF16)

    w_r = jnp.concatenate([w_router_grp[l], w_router_exp[l],
                           jnp.zeros((D_MODEL, LANES - N_GROUPS - N_EXPERTS), F32)], axis=1)
    w_r_hi = w_r.astype(BF16)
    w_r_lo = (w_r - w_r_hi.astype(F32)).astype(BF16)
    b_r = jnp.concatenate([b_router_grp[l], b_router_exp[l],
                           jnp.zeros((LANES - N_GROUPS - N_EXPERTS,), F32)]).reshape(1, LANES)

    pad64 = jnp.zeros((LANES - D_B,), F32)
    lamv = jnp.stack([jnp.concatenate([v[l], pad64]) for v in (lambda_q1, lambda_k1, lambda_q2, lambda_k2)])

    return {
        "g_mix": g_mix[l].reshape(1, D_MODEL), "w_main": w_main,
        "g_q_lat": g_q_lat[l].reshape(1, Q_LORA), "w_uq": uq_pad,
        "g_kv_lat": g_kv_lat[l].reshape(1, KV_LORA), "w_uk": uk_pad, "w_uv": w_uv[l].astype(BF16),
        "gvecs": gvecs, "bd": bd, "lamv": lamv, "g_diff_sub": g_diff_sub[l].reshape(V_B, 1),
        "w_out": w_out[l].astype(BF16), "g_ffn": g_ffn[l].reshape(1, D_MODEL),
        "w_r_hi": w_r_hi, "w_r_lo": w_r_lo, "b_r": b_r,
        "w_exp_in": w_exp_in[l].astype(BF16), "w_exp_out": w_exp_out[l].astype(BF16),
        "g_ple": g_ple[l].reshape(1, D_MODEL), "w_ple_gate": w_ple_gate[l].astype(BF16),
        "w_ple_proj": w_ple_proj[l].astype(BF16),
    }


def _layer(x, pe, pos, past, wp, lam_init):
    b, s, _ = x.shape
    tab_a = _rope_tables(pos, QK_ROPE, (0,))
    tab_b = _rope_tables(pos, ROT_B, (0, D_B))
    q_cat, ckv, kr, dq, dk32, dk16, dv32, dv16, ga, gb = _proj(x, tab_a, tab_b, wp)

    if past is None:
        ckv_all, kr_all, dk_all, dv_all = ckv, kr, dk16, dv16
        q_off = 0
        tq = min(256, s)
    else:
        ckv_p, kr_p, dk_p, dv_p = past
        past_len = ckv_p.shape[1]
        ckv_all = jnp.concatenate([ckv_p, ckv], axis=1)
        kr_all = jnp.concatenate([jnp.pad(kr_p, ((0, 0), (0, 0), (0, LANES - QK_ROPE))), kr], axis=1)
        dk_all = jnp.concatenate([dk_p.reshape(b, past_len, D_MODEL).astype(BF16), dk16], axis=1)
        dv_all = jnp.concatenate([dv_p.reshape(b, past_len, D_MODEL).astype(BF16), dv16], axis=1)
        q_off = past_len
        tq = s
    sk = ckv_all.shape[1]

    k_cat, v_a = _kv_up(ckv_all.reshape(b * sk, KV_LORA), kr_all.reshape(b * sk, LANES), wp)
    o_a = _attention(q_cat, k_cat.reshape(b, sk, D_MODEL), v_a.reshape(b, sk, D_MODEL), wp["lamv"], wp["g_diff_sub"],
                     maps=1, q_off=q_off, lam_init=lam_init, tq=tq)
    o_b = _attention(dq, dk_all, dv_all, wp["lamv"], wp["g_diff_sub"],
                     maps=2, q_off=q_off, lam_init=lam_init, tq=tq)

    n = b * s
    flat = lambda a: a.reshape(n, a.shape[-1])
    h, hn, route = _merge(flat(x), flat(o_a), flat(o_b), flat(ga), flat(gb), wp)

    n_tiles = (2 * n) // MOE_TILE + N_EXPERTS
    tile_expert, n_used, src_tok, slot_of = _routing_tables(route, n_tiles)
    y = _moe(hn, tile_expert, n_used, src_tok, wp)
    out = _combine(route, h, flat(pe), y, slot_of, wp)

    return (out.reshape(b, s, D_MODEL),
            (ckv, kr[:, :, :QK_ROPE], dk32.reshape(b, s, H_B, 2, D_B), dv32.reshape(b, s, H_B, V_B)))


def kernel(x_prompt, x_sample, p_prompt, p_sample, cache_mla_ckv, cache_mla_krope, cache_diff_k, cache_diff_v,
           g_mix, w_in, g_q_lat, w_uq, g_kv_lat, w_uk, w_uv, g_mla_qn, g_mla_qr, g_mla_kn, g_mla_kr,
           g_diff_q, g_diff_k, lambda_q1, lambda_k1, lambda_q2, lambda_k2, g_diff_sub, w_out,
           g_ffn, w_router_grp, b_router_grp, w_router_exp, b_router_exp, w_exp_in, w_exp_out,
           g_ple, w_ple_gate, w_ple_proj):
    depth = w_in.shape[0]
    pos_p = jnp.arange(x_prompt.shape[1], dtype=jnp.int32)
    pos_s = cache_mla_ckv.shape[2] + jnp.arange(x_sample.shape[1], dtype=jnp.int32)
    hp, hs = x_prompt, x_sample
    st_p, st_s = [], []
    for l in range(depth):
        wp = _prep_weights(l, g_mix, w_in, g_q_lat, w_uq, g_kv_lat, w_uk, w_uv, g_mla_qn, g_mla_qr, g_mla_kn,
                           g_mla_kr, g_diff_q, g_diff_k, lambda_q1, lambda_k1, lambda_q2, lambda_k2, g_diff_sub,
                           w_out, g_ffn, w_router_grp, b_router_grp, w_router_exp, b_router_exp, w_exp_in,
                           w_exp_out, g_ple, w_ple_gate, w_ple_proj)
        lam_init = 0.8 - 0.6 * math.exp(-0.3 * l)
        hp, sp = _layer(hp, p_prompt[l], pos_p, None, wp, lam_init)
        hs, ss = _layer(hs, p_sample[l], pos_s,
                        (cache_mla_ckv[l], cache_mla_krope[l], cache_diff_k[l], cache_diff_v[l]), wp, lam_init)
        st_p.append(sp)
        st_s.append(ss)
    stack = lambda sts, j: jnp.stack([st[j] for st in sts])
    return (hp, hs,
            stack(st_p, 0), stack(st_p, 1), stack(st_p, 2), stack(st_p, 3),
            stack(st_s, 0), stack(st_s, 1), stack(st_s, 2), stack(st_s, 3))
```

```python
import functools
import math

import jax
import jax.numpy as jnp
from jax import lax
from jax.experimental import pallas as pl
from jax.experimental.pallas import tpu as pltpu

F32 = jnp.float32
BF16 = jnp.bfloat16

D_MODEL = 1024
CHUNK = 64
ROPE_THETA = 500000.0
EPS = 1e-6
H_A = 8
Q_LORA = 384
KV_LORA = 256
QK_NOPE = 64
QK_ROPE = 32
V_A = 128
MLA_SCALE = 1.0 / math.sqrt(QK_NOPE + QK_ROPE)
H_B = 8
D_B = 64
V_B = 2 * D_B
ROT_B = D_B // 4
DIFF_SCALE = 1.0 / math.sqrt(D_B)
LOG2E = math.log2(math.e)
N_GROUPS = 4
EXP_PER_GROUP = 8
N_EXPERTS = N_GROUPS * EXP_PER_GROUP
D_EXPERT = 256
D_PLE = 256
IN_SIZES = (Q_LORA, KV_LORA, QK_ROPE, H_B * 2 * D_B, H_B * 2 * D_B, H_B * V_B, D_MODEL, D_MODEL)

LANES = 128
N_HEADS = 8
VMEM_LIMIT = 56 * 1024 * 1024

_O_Q = 0
_O_CKV = _O_Q + Q_LORA
_O_KR = _O_CKV + KV_LORA
_O_DQ = _O_KR + LANES
_O_DK = _O_DQ + D_MODEL
_O_DV = _O_DK + D_MODEL
_O_GA = _O_DV + D_MODEL
_O_GB = _O_GA + D_MODEL
_W_MAIN_COLS = _O_GB + D_MODEL


def _dot(a, b):
    return jnp.dot(a, b, preferred_element_type=F32)


def _sigmoid(x):
    return 1.0 / (1.0 + jnp.exp(-x))


def _row_rms(x, g):
    ms = jnp.mean(x * x, axis=-1, keepdims=True)
    return (x * lax.rsqrt(ms + EPS)) * g


def _seg_rms(z, bd, g):
    ms = _dot((z * z).astype(BF16), bd)
    return (z * lax.rsqrt(ms + EPS)) * g


def _rope_block(y, tab_ref, half):
    return (y * tab_ref[0]
            + pltpu.roll(y, LANES - half, 1) * tab_ref[1]
            + pltpu.roll(y, half, 1) * tab_ref[2])


def _proj_kernel(x_ref, taba_ref, tabb_ref, gmix_ref, w_ref, gql_ref, wuq_ref, gkv_ref, gv_ref, bd_ref,
                 qcat_ref, ckv_ref, kr_ref, dq_ref, dk32_ref, dk16_ref, dv32_ref, dv16_ref, ga_ref, gb_ref):
    xn = _row_rms(x_ref[...], gmix_ref[...]).astype(BF16)
    bd_a = bd_ref[0]
    bd_b = bd_ref[1]

    ql = _row_rms(_dot(xn, w_ref[:, _O_Q:_O_Q + Q_LORA]), gql_ref[...]).astype(BF16)
    q = _dot(ql, wuq_ref[...])
    for h in range(N_HEADS):
        sl = slice(h * LANES, (h + 1) * LANES)
        qb = _rope_block(_seg_rms(q[:, sl], bd_a, gv_ref[0:1, :]), taba_ref, QK_ROPE // 2)
        qcat_ref[:, sl] = (qb * (MLA_SCALE * LOG2E)).astype(BF16)

    ckv_ref[...] = _row_rms(_dot(xn, w_ref[:, _O_CKV:_O_CKV + KV_LORA]), gkv_ref[...])
    kr = _dot(xn, w_ref[:, _O_KR:_O_KR + LANES])
    kr_ref[...] = _rope_block(_seg_rms(kr, bd_a, gv_ref[1:2, :]), taba_ref, QK_ROPE // 2)

    zq = _dot(xn, w_ref[:, _O_DQ:_O_DQ + D_MODEL])
    zk = _dot(xn, w_ref[:, _O_DK:_O_DK + D_MODEL])
    for h in range(N_HEADS):
        sl = slice(h * LANES, (h + 1) * LANES)
        qb = _rope_block(_seg_rms(zq[:, sl], bd_b, gv_ref[2:3, :]), tabb_ref, ROT_B // 2)
        dq_ref[:, sl] = (qb * (DIFF_SCALE * LOG2E)).astype(BF16)
        kb = _rope_block(_seg_rms(zk[:, sl], bd_b, gv_ref[3:4, :]), tabb_ref, ROT_B // 2)
        dk32_ref[:, sl] = kb
        dk16_ref[:, sl] = kb.astype(BF16)

    dv = _dot(xn, w_ref[:, _O_DV:_O_DV + D_MODEL])
    dv32_ref[...] = dv
    dv16_ref[...] = dv.astype(BF16)
    ga_ref[...] = _sigmoid(_dot(xn, w_ref[:, _O_GA:_O_GA + D_MODEL])).astype(BF16)
    gb_ref[...] = _sigmoid(_dot(xn, w_ref[:, _O_GB:_O_GB + D_MODEL])).astype(BF16)


def _const_spec(shape):
    nd = len(shape)
    return pl.BlockSpec(shape, lambda *_: (0,) * nd, pipeline_mode=pl.Buffered(1))


def _proj(x, tab_a, tab_b, wp):
    b, s, _ = x.shape
    tm = min(512, s)
    grid = (b, s // tm)

    def tok(width):
        return pl.BlockSpec((None, tm, width), lambda bi, i: (bi, i, 0))

    tab_spec = pl.BlockSpec((3, tm, LANES), lambda bi, i: (0, i, 0))
    out_shapes = [
        jax.ShapeDtypeStruct((b, s, D_MODEL), BF16),
        jax.ShapeDtypeStruct((b, s, KV_LORA), F32),
        jax.ShapeDtypeStruct((b, s, LANES), F32),
        jax.ShapeDtypeStruct((b, s, D_MODEL), BF16),
        jax.ShapeDtypeStruct((b, s, D_MODEL), F32),
        jax.ShapeDtypeStruct((b, s, D_MODEL), BF16),
        jax.ShapeDtypeStruct((b, s, D_MODEL), F32),
        jax.ShapeDtypeStruct((b, s, D_MODEL), BF16),
        jax.ShapeDtypeStruct((b, s, D_MODEL), BF16),
        jax.ShapeDtypeStruct((b, s, D_MODEL), BF16),
    ]
    out_specs = [tok(D_MODEL), tok(KV_LORA), tok(LANES), tok(D_MODEL), tok(D_MODEL), tok(D_MODEL),
                 tok(D_MODEL), tok(D_MODEL), tok(D_MODEL), tok(D_MODEL)]
    in_specs = [
        tok(D_MODEL), tab_spec, tab_spec,
        _const_spec((1, D_MODEL)), _const_spec((D_MODEL, _W_MAIN_COLS)),
        _const_spec((1, Q_LORA)), _const_spec((Q_LORA, D_MODEL)), _const_spec((1, KV_LORA)),
        _const_spec((8, LANES)), _const_spec((2, LANES, LANES)),
    ]
    return pl.pallas_call(
        _proj_kernel, grid=grid, in_specs=in_specs, out_specs=out_specs, out_shape=out_shapes,
        compiler_params=pltpu.CompilerParams(dimension_semantics=("parallel", "parallel"),
                                             vmem_limit_bytes=VMEM_LIMIT),
        name="proj",
    )(x, tab_a, tab_b, wp["g_mix"], wp["w_main"], wp["g_q_lat"], wp["w_uq"], wp["g_kv_lat"],
      wp["gvecs"], wp["bd"])


def _kvup_kernel(ckv_ref, kr_ref, wuk_ref, wuv_ref, gv_ref, bd_ref, kcat_ref, va_ref):
    c = ckv_ref[...].astype(BF16)
    kn = _dot(c, wuk_ref[...])
    kr = kr_ref[...]
    bd_a = bd_ref[0]
    for h in range(N_HEADS):
        sl = slice(h * LANES, (h + 1) * LANES)
        kcat_ref[:, sl] = (_seg_rms(kn[:, sl], bd_a, gv_ref[4:5, :]) + kr).astype(BF16)
    va_ref[...] = _dot(c, wuv_ref[...]).astype(BF16)


def _kv_up(ckv, kr, wp):
    m = ckv.shape[0]
    tm = 256
    assert m % tm == 0

    def tok(width):
        return pl.BlockSpec((tm, width), lambda i: (i, 0))

    return pl.pallas_call(
        _kvup_kernel, grid=(m // tm,),
        in_specs=[tok(KV_LORA), tok(LANES), _const_spec((KV_LORA, D_MODEL)), _const_spec((KV_LORA, D_MODEL)),
                  _const_spec((8, LANES)), _const_spec((2, LANES, LANES))],
        out_specs=[tok(D_MODEL), tok(D_MODEL)],
        out_shape=[jax.ShapeDtypeStruct((m, D_MODEL), BF16), jax.ShapeDtypeStruct((m, D_MODEL), BF16)],
        compiler_params=pltpu.CompilerParams(dimension_semantics=("parallel",), vmem_limit_bytes=VMEM_LIMIT),
        name="kv_up",
    )(ckv, kr, wp["w_uk"], wp["w_uv"], wp["gvecs"], wp["bd"])


def _attn_kernel(lamv_ref, gsub_ref, q_ref, k_ref, v_ref, o_ref, qt_sc, m_sc, l_sc, acc_sc, st0_sc, st1_sc,
                 *, maps, tq, q_off, lam_init):
    qi = pl.program_id(1)
    rows = maps * tq
    n_full = q_off // tq + qi

    eye = (lax.broadcasted_iota(jnp.int32, (LANES, LANES), 0)
           == lax.broadcasted_iota(jnp.int32, (LANES, LANES), 1)).astype(BF16)
    for h in range(N_HEADS):
        sl = slice(h * LANES, (h + 1) * LANES)
        qt = lax.dot_general(eye, q_ref[:, sl], (((1,), (1,)), ((), ())), preferred_element_type=F32).astype(BF16)
        if maps == 2:
            sub = lax.broadcasted_iota(jnp.int32, (LANES, tq), 0)
            zero = jnp.zeros_like(qt)
            qt = jnp.concatenate([jnp.where(sub < D_B, qt, zero), jnp.where(sub >= D_B, qt, zero)], axis=1)
        qt_sc[h] = qt
        m_sc[h] = jnp.full((1, rows), -jnp.inf, F32)
        l_sc[h] = jnp.zeros((1, rows), F32)
        acc_sc[h] = jnp.zeros((LANES, rows), F32)

    def scores(h, koff, buf):
        buf[h] = _dot(k_ref[pl.ds(koff, tq), h * LANES:(h + 1) * LANES], qt_sc[h])

    def softmax_pv(h, koff, buf, vis):
        sl = slice(h * LANES, (h + 1) * LANES)
        st = buf[h]
        if vis is not None:
            st = jnp.where(vis, st, -jnp.inf)
        m_prev = m_sc[h]
        m_next = jnp.maximum(m_prev, jnp.max(st, axis=0, keepdims=True))
        alpha = jnp.exp2(m_prev - m_next)
        pt = jnp.exp2(st - m_next)
        l_sc[h] = alpha * l_sc[h] + jnp.sum(pt, axis=0, keepdims=True)
        pv = lax.dot_general(v_ref[pl.ds(koff, tq), sl], pt.astype(BF16), (((0,), (0,)), ((), ())),
                             preferred_element_type=F32)
        acc_sc[h] = alpha * acc_sc[h] + pv
        m_sc[h] = m_next

    for h in range(N_HEADS):
        scores(h, 0, st0_sc)

    def step(c, cur, nxt):
        koff = pl.multiple_of(c * tq, tq)
        knext = pl.multiple_of((c + 1) * tq, tq)
        scores(0, knext, nxt)
        for h in range(N_HEADS):
            if h + 1 < N_HEADS:
                scores(h + 1, knext, nxt)
            softmax_pv(h, koff, cur, None)

    def chunk_pair(i, carry):
        step(2 * i, st0_sc, st1_sc)
        step(2 * i + 1, st1_sc, st0_sc)
        return carry

    lax.fori_loop(0, n_full // 2, chunk_pair, 0)
    odd = n_full % 2 == 1

    @pl.when(odd)
    def _odd_tail():
        step(n_full - 1, st0_sc, st1_sc)

    kc = lax.broadcasted_iota(jnp.int32, (tq, rows), 0) >> 6
    qc = lax.broadcasted_iota(jnp.int32, (tq, rows), 1)
    if maps == 2:
        qc = jnp.where(qc >= tq, qc - tq, qc)
    vis = kc <= (qc >> 6)
    kdiag = pl.multiple_of(n_full * tq, tq)

    @pl.when(odd)
    def _diag_odd():
        for h in range(N_HEADS):
            softmax_pv(h, kdiag, st1_sc, vis)

    @pl.when(jnp.logical_not(odd))
    def _diag_even():
        for h in range(N_HEADS):
            softmax_pv(h, kdiag, st0_sc, vis)

    if maps == 2:
        lv = lamv_ref[...]
        lam = (jnp.exp(jnp.sum(lv[0:1] * lv[1:2], axis=-1, keepdims=True))
               - jnp.exp(jnp.sum(lv[2:3] * lv[3:4], axis=-1, keepdims=True)) + lam_init)
    eye_q = (lax.broadcasted_iota(jnp.int32, (tq, tq), 0)
             == lax.broadcasted_iota(jnp.int32, (tq, tq), 1)).astype(BF16)
    for h in range(N_HEADS):
        sl = slice(h * LANES, (h + 1) * LANES)
        ot = acc_sc[h] / l_sc[h]
        if maps == 2:
            ot = ot[:, :tq] - lam * ot[:, tq:]
            ms = jnp.mean(ot * ot, axis=0, keepdims=True)
            ot = ((ot * lax.rsqrt(ms + EPS)) * gsub_ref[...]) * (1.0 - lam_init)
        o = lax.dot_general(eye_q, ot.astype(BF16), (((1,), (1,)), ((), ())), preferred_element_type=F32)
        o_ref[:, sl] = o.astype(o_ref.dtype)


def _attention(q, k, v, lamv, gsub, *, maps, q_off, lam_init, tq):
    b, sq, _ = q.shape
    sk = k.shape[1]
    assert sq % tq == 0 and q_off % tq == 0 and tq % CHUNK == 0 and sk == q_off + sq
    rows = maps * tq
    kern = functools.partial(_attn_kernel, maps=maps, tq=tq, q_off=q_off, lam_init=lam_init)
    kv_spec = pl.BlockSpec((None, sk, D_MODEL), lambda bi, qi: (bi, 0, 0), pipeline_mode=pl.Buffered(1))
    return pl.pallas_call(
        kern, grid=(b, sq // tq),
        in_specs=[_const_spec((4, LANES)), _const_spec((LANES, 1)),
                  pl.BlockSpec((None, tq, D_MODEL), lambda bi, qi: (bi, qi, 0)), kv_spec, kv_spec],
        out_specs=pl.BlockSpec((None, tq, D_MODEL), lambda bi, qi: (bi, qi, 0)),
        out_shape=jax.ShapeDtypeStruct((b, sq, D_MODEL), BF16),
        scratch_shapes=[pltpu.VMEM((N_HEADS, LANES, rows), BF16), pltpu.VMEM((N_HEADS, 1, rows), F32),
                        pltpu.VMEM((N_HEADS, 1, rows), F32), pltpu.VMEM((N_HEADS, LANES, rows), F32),
                        pltpu.VMEM((N_HEADS, tq, rows), F32), pltpu.VMEM((N_HEADS, tq, rows), F32)],
        compiler_params=pltpu.CompilerParams(dimension_semantics=("parallel", "arbitrary"),
                                             vmem_limit_bytes=VMEM_LIMIT),
        name="attn_diff" if maps == 2 else "attn_mla",
    )(lamv, gsub, q, k, v)


def _merge_kernel(x_ref, oa_ref, ob_ref, ga_ref, gb_ref, wout_ref, gffn_ref, wrh_ref, wrl_ref, br_ref,
                  h_ref, hn_ref, route_ref):
    merged = (ga_ref[...].astype(F32) * oa_ref[...].astype(F32)
              + gb_ref[...].astype(F32) * ob_ref[...].astype(F32))
    h = x_ref[...] + _dot(merged.astype(BF16), wout_ref[...])
    h_ref[...] = h
    hn = _row_rms(h, gffn_ref[...])
    hn_ref[...] = hn

    hi = hn.astype(BF16)
    lo = (hn - hi.astype(F32)).astype(BF16)
    logits = _dot(hi, wrh_ref[...]) + (_dot(lo, wrh_ref[...]) + _dot(hi, wrl_ref[...])) + br_ref[...]
    lane = lax.broadcasted_iota(jnp.int32, logits.shape, 1).astype(F32)
    neg = jnp.full_like(logits, -jnp.inf)
    big = jnp.full_like(logits, 1e9)

    def first_argmax(vals):
        vmax = jnp.max(vals, axis=1, keepdims=True)
        return vmax, jnp.min(jnp.where(vals == vmax, lane, big), axis=1, keepdims=True)

    gmask = lane < N_GROUPS
    gmax, gsel = first_argmax(jnp.where(gmask, logits, neg))
    p_top = 1.0 / jnp.sum(jnp.where(gmask, jnp.exp(logits - gmax), 0.0), axis=1, keepdims=True)
    e_lo = N_GROUPS + gsel * EXP_PER_GROUP
    emask = jnp.logical_and(lane >= e_lo, lane < e_lo + EXP_PER_GROUP)
    le = jnp.where(emask, logits, neg)
    v1, i1 = first_argmax(le)
    v2, i2 = first_argmax(jnp.where(lane == i1, neg, le))
    t = jnp.exp(v2 - v1)
    w1 = p_top / (1.0 + t)
    w2 = p_top * t / (1.0 + t)
    route = jnp.where(lane == 0, i1 - N_GROUPS, 0.0)
    route = jnp.where(lane == 1, i2 - N_GROUPS, route)
    route = jnp.where(lane == 2, w1, route)
    route = jnp.where(lane == 3, w2, route)
    route_ref[...] = route


def _merge(x, oa, ob, ga, gb, wp):
    n = x.shape[0]
    tm = 256
    assert n % tm == 0

    def tok(width):
        return pl.BlockSpec((tm, width), lambda i: (i, 0))

    return pl.pallas_call(
        _merge_kernel, grid=(n // tm,),
        in_specs=[tok(D_MODEL)] * 5 + [_const_spec((D_MODEL, D_MODEL)), _const_spec((1, D_MODEL)),
                                       _const_spec((D_MODEL, LANES)), _const_spec((D_MODEL, LANES)),
                                       _const_spec((1, LANES))],
        out_specs=[tok(D_MODEL), tok(D_MODEL), tok(LANES)],
        out_shape=[jax.ShapeDtypeStruct((n, D_MODEL), F32), jax.ShapeDtypeStruct((n, D_MODEL), F32),
                   jax.ShapeDtypeStruct((n, LANES), F32)],
        compiler_params=pltpu.CompilerParams(dimension_semantics=("parallel",), vmem_limit_bytes=VMEM_LIMIT),
        name="merge",
    )(x, oa, ob, ga, gb, wp["w_out"], wp["g_ffn"], wp["w_r_hi"], wp["w_r_lo"], wp["b_r"])


MOE_TILE = 256


def _gather_rows(idx_ref, n_rows, src_hbm, dst_ref, sem, *, unrolled):
    def start(r):
        pltpu.make_async_copy(src_hbm.at[pl.ds(idx_ref[0, 0, r], 1), :], dst_ref.at[pl.ds(r, 1), :], sem).start()

    if unrolled:
        for r in range(n_rows):
            start(r)
    else:
        def body(r, carry):
            start(r)
            return carry
        lax.fori_loop(0, n_rows, body, 0, unroll=8)


def _wait_rows(n_rows, src_hbm, dst_ref, sem):
    pltpu.make_async_copy(src_hbm.at[pl.ds(0, n_rows), :], dst_ref, sem).wait()


def _moe_kernel(texp_ref, nused_ref, idx_ref, idxn_ref, hn_hbm, wei_ref, weo_ref, y_ref, xbuf, sem):
    del texp_ref
    i = pl.program_id(0)
    n_used = nused_ref[0]
    slot = i % 2

    @pl.when(jnp.logical_and(i == 0, n_used > 0))
    def _first():
        _gather_rows(idx_ref, MOE_TILE, hn_hbm, xbuf.at[0], sem.at[0], unrolled=False)

    @pl.when(i + 1 < n_used)
    def _prefetch():
        _gather_rows(idxn_ref, MOE_TILE, hn_hbm, xbuf.at[1 - slot], sem.at[1 - slot], unrolled=True)

    @pl.when(i < n_used)
    def _compute():
        _wait_rows(MOE_TILE, hn_hbm, xbuf.at[slot], sem.at[slot])
        x = xbuf[slot].astype(BF16)
        ab = _dot(x, wei_ref[...])
        a = ab[:, :D_EXPERT]
        act = (a * _sigmoid(a)) * ab[:, D_EXPERT:]
        y_ref[...] = _dot(act.astype(BF16), weo_ref[...])

    @pl.when(i >= n_used)
    def _idle():
        y_ref[...] = jnp.zeros(y_ref.shape, F32)


def _moe(hn, tile_expert, n_used, src_tok, wp):
    n_tiles = tile_expert.shape[0]
    idx3 = src_tok.reshape(n_tiles, 1, MOE_TILE)
    idx_spec = pl.BlockSpec((1, 1, MOE_TILE), lambda i, te, nu: (i, 0, 0), memory_space=pltpu.SMEM)
    idxn_spec = pl.BlockSpec((1, 1, MOE_TILE), lambda i, te, nu: (jnp.minimum(i + 1, n_tiles - 1), 0, 0),
                             memory_space=pltpu.SMEM)
    grid_spec = pltpu.PrefetchScalarGridSpec(
        num_scalar_prefetch=2, grid=(n_tiles,),
        in_specs=[idx_spec, idxn_spec, pl.BlockSpec(memory_space=pl.ANY),
                  pl.BlockSpec((None, D_MODEL, 2 * D_EXPERT), lambda i, te, nu: (te[i], 0, 0)),
                  pl.BlockSpec((None, D_EXPERT, D_MODEL), lambda i, te, nu: (te[i], 0, 0))],
        out_specs=pl.BlockSpec((MOE_TILE, D_MODEL), lambda i, te, nu: (i, 0)),
        scratch_shapes=[pltpu.VMEM((2, MOE_TILE, D_MODEL), F32), pltpu.SemaphoreType.DMA((2,))],
    )
    return pl.pallas_call(
        _moe_kernel, grid_spec=grid_spec,
        out_shape=jax.ShapeDtypeStruct((n_tiles * MOE_TILE, D_MODEL), F32),
        compiler_params=pltpu.CompilerParams(dimension_semantics=("arbitrary",), vmem_limit_bytes=VMEM_LIMIT),
        name="moe",
    )(tile_expert, n_used, idx3, idx3, hn, wp["w_exp_in"], wp["w_exp_out"])


def _routing_tables(route, n_tiles):
    n = route.shape[0]
    e = route[:, 0:2].astype(jnp.int32).reshape(-1)
    onehot = (e[:, None] == jnp.arange(N_EXPERTS, dtype=jnp.int32)[None, :]).astype(jnp.int32)
    rank = jnp.sum((jnp.cumsum(onehot, axis=0) - onehot) * onehot, axis=1)
    counts = jnp.sum(onehot, axis=0)
    tiles_per = (counts + MOE_TILE - 1) // MOE_TILE
    tile_end = jnp.cumsum(tiles_per)
    start = (tile_end - tiles_per) * MOE_TILE
    pos = start[e] + rank
    filler = jnp.arange(n_tiles * MOE_TILE, dtype=jnp.int32) % n
    src_tok = filler.at[pos].set(jnp.arange(2 * n, dtype=jnp.int32) // 2)
    n_used = tile_end[-1]
    tile_ids = jnp.minimum(jnp.arange(n_tiles, dtype=jnp.int32), n_used - 1)
    tile_expert = jnp.sum((tile_ids[:, None] >= tile_end[None, :]).astype(jnp.int32), axis=1)
    return tile_expert.astype(jnp.int32), n_used.reshape(1).astype(jnp.int32), src_tok, pos.reshape(n, 2)


COMB_TILE = 256


def _combine_kernel(pos_ref, posn_ref, route_ref, h_ref, pe_ref, y_hbm, gple_ref, wg_ref, wp_ref,
                    o_ref, ybuf, sem):
    i = pl.program_id(0)
    n = pl.num_programs(0)
    slot = i % 2
    rows = 2 * COMB_TILE

    @pl.when(i == 0)
    def _first():
        _gather_rows(pos_ref, rows, y_hbm, ybuf.at[0], sem.at[0], unrolled=False)

    @pl.when(i + 1 < n)
    def _prefetch():
        _gather_rows(posn_ref, rows, y_hbm, ybuf.at[1 - slot], sem.at[1 - slot], unrolled=True)

    _wait_rows(rows, y_hbm, ybuf.at[slot], sem.at[slot])
    route = route_ref[...]
    h = h_ref[...] + (route[:, 2:3] * ybuf[slot, 0:COMB_TILE, :] + route[:, 3:4] * ybuf[slot, COMB_TILE:rows, :])
    gate = _sigmoid(_dot(_row_rms(h, gple_ref[...]).astype(BF16), wg_ref[...]))
    o_ref[...] = h + gate * _dot(pe_ref[...].astype(BF16), wp_ref[...])


def _combine(route, h, pe, y, pos, wp):
    n = h.shape[0]
    assert n % COMB_TILE == 0
    n_tiles = n // COMB_TILE
    pos3 = pos.reshape(n_tiles, COMB_TILE, 2).transpose(0, 2, 1).reshape(n_tiles, 1, 2 * COMB_TILE)

    def tok(width):
        return pl.BlockSpec((COMB_TILE, width), lambda i: (i, 0))

    pos_spec = pl.BlockSpec((1, 1, 2 * COMB_TILE), lambda i: (i, 0, 0), memory_space=pltpu.SMEM)
    posn_spec = pl.BlockSpec((1, 1, 2 * COMB_TILE), lambda i: (jnp.minimum(i + 1, n_tiles - 1), 0, 0),
                             memory_space=pltpu.SMEM)
    return pl.pallas_call(
        _combine_kernel, grid=(n_tiles,),
        in_specs=[pos_spec, posn_spec, tok(LANES), tok(D_MODEL), tok(D_PLE), pl.BlockSpec(memory_space=pl.ANY),
                  _const_spec((1, D_MODEL)), _const_spec((D_MODEL, D_MODEL)), _const_spec((D_PLE, D_MODEL))],
        out_specs=tok(D_MODEL),
        out_shape=jax.ShapeDtypeStruct((n, D_MODEL), F32),
        scratch_shapes=[pltpu.VMEM((2, 2 * COMB_TILE, D_MODEL), F32), pltpu.SemaphoreType.DMA((2,))],
        compiler_params=pltpu.CompilerParams(dimension_semantics=("arbitrary",), vmem_limit_bytes=VMEM_LIMIT),
        name="combine",
    )(pos3, pos3, route, h, pe, y, wp["g_ple"], wp["w_ple_gate"], wp["w_ple_proj"])


def _rope_tables(pos, rot, offsets):
    half = rot // 2
    inv = ROPE_THETA ** (-jnp.arange(half, dtype=F32) * (2.0 / rot))
    lane = jnp.arange(LANES)
    rel = jnp.full((LANES,), -1)
    for o in offsets:
        rel = jnp.where((lane >= o) & (lane < o + rot), lane - o, rel)
    first, second = (rel >= 0) & (rel < half), rel >= half
    ang = pos.astype(F32)[:, None] * inv[jnp.maximum(rel, 0) % half][None, :]
    c, s = jnp.cos(ang), jnp.sin(ang)
    return jnp.stack([jnp.where(rel >= 0, c, 1.0), jnp.where(first, -s, 0.0), jnp.where(second, s, 0.0)])


def _prep_weights(l, g_mix, w_in, g_q_lat, w_uq, g_kv_lat, w_uk, w_uv, g_mla_qn, g_mla_qr, g_mla_kn, g_mla_kr,
                  g_diff_q, g_diff_k, lambda_q1, lambda_k1, lambda_q2, lambda_k2, g_diff_sub, w_out,
                  g_ffn, w_router_grp, b_router_grp, w_router_exp, b_router_exp, w_exp_in, w_exp_out,
                  g_ple, w_ple_gate, w_ple_proj):
    offs = [0]
    for sz in IN_SIZES:
        offs.append(offs[-1] + sz)
    wi = w_in[l]
    seg = [wi[:, offs[j]:offs[j + 1]] for j in range(len(IN_SIZES))]
    w_kr_pad = jnp.pad(seg[2], ((0, 0), (0, LANES - QK_ROPE)))
    w_main = jnp.concatenate([seg[0], seg[1], w_kr_pad] + seg[3:], axis=1).astype(BF16)

    hd = QK_NOPE + QK_ROPE
    uq = w_uq[l].reshape(Q_LORA, H_A, hd)
    uq_pad = jnp.concatenate([uq[:, :, QK_NOPE:], jnp.zeros((Q_LORA, H_A, LANES - hd), F32), uq[:, :, :QK_NOPE]],
                             axis=2).reshape(Q_LORA, H_A * LANES).astype(BF16)
    uk = w_uk[l].reshape(KV_LORA, H_A, QK_NOPE)
    uk_pad = jnp.concatenate([jnp.zeros((KV_LORA, H_A, LANES - QK_NOPE), F32), uk], axis=2)
    uk_pad = uk_pad.reshape(KV_LORA, H_A * LANES).astype(BF16)

    z32 = jnp.zeros((LANES - hd,), F32)
    gvecs = jnp.stack([
        jnp.concatenate([g_mla_qr[l], z32, g_mla_qn[l]]),
        jnp.concatenate([g_mla_kr[l], jnp.zeros((LANES - QK_ROPE,), F32)]),
        jnp.concatenate([g_diff_q[l], g_diff_q[l]]),
        jnp.concatenate([g_diff_k[l], g_diff_k[l]]),
        jnp.concatenate([jnp.zeros((LANES - QK_NOPE,), F32), g_mla_kn[l]]),
        jnp.zeros((LANES,), F32), jnp.zeros((LANES,), F32), jnp.zeros((LANES,), F32)])

    lane = jnp.arange(LANES)
    seg_a = jnp.where(lane < QK_ROPE, 0, jnp.where(lane < LANES - QK_NOPE, -1, 1))
    same_a = (seg_a[:, None] == seg_a[None, :]) & (seg_a[:, None] >= 0)
    bd_a = jnp.where(same_a, jnp.where(seg_a[:, None] == 0, 1.0 / QK_ROPE, 1.0 / QK_NOPE), 0.0)
    seg_b = lane // D_B
    bd_b = jnp.where(seg_b[:, None] == seg_b[None, :], 1.0 / D_B, 0.0)
    bd = jnp.stack([bd_a, bd_b]).astype(BF16)

    w_r = jnp.concatenate([w_router_grp[l], w_router_exp[l],
                           jnp.zeros((D_MODEL, LANES - N_GROUPS - N_EXPERTS), F32)], axis=1)
    w_r_hi = w_r.astype(BF16)
    w_r_lo = (w_r - w_r_hi.astype(F32)).astype(BF16)
    b_r = jnp.concatenate([b_router_grp[l], b_router_exp[l],
                           jnp.zeros((LANES - N_GROUPS - N_EXPERTS,), F32)]).reshape(1, LANES)

    pad64 = jnp.zeros((LANES - D_B,), F32)
    lamv = jnp.stack([jnp.concatenate([v[l], pad64]) for v in (lambda_q1, lambda_k1, lambda_q2, lambda_k2)])

    return {
        "g_mix": g_mix[l].reshape(1, D_MODEL), "w_main": w_main,
        "g_q_lat": g_q_lat[l].reshape(1, Q_LORA), "w_uq": uq_pad,
        "g_kv_lat": g_kv_lat[l].reshape(1, KV_LORA), "w_uk": uk_pad, "w_uv": w_uv[l].astype(BF16),
        "gvecs": gvecs, "bd": bd, "lamv": lamv, "g_diff_sub": g_diff_sub[l].reshape(V_B, 1),
        "w_out": w_out[l].astype(BF16), "g_ffn": g_ffn[l].reshape(1, D_MODEL),
        "w_r_hi": w_r_hi, "w_r_lo": w_r_lo, "b_r": b_r,
        "w_exp_in": w_exp_in[l].astype(BF16), "w_exp_out": w_exp_out[l].astype(BF16),
        "g_ple": g_ple[l].reshape(1, D_MODEL), "w_ple_gate": w_ple_gate[l].astype(BF16),
        "w_ple_proj": w_ple_proj[l].astype(BF16),
    }


def _layer(x, pe, pos, past, wp, lam_init):
    b, s, _ = x.shape
    tab_a = _rope_tables(pos, QK_ROPE, (0,))
    tab_b = _rope_tables(pos, ROT_B, (0, D_B))
    q_cat, ckv, kr, dq, dk32, dk16, dv32, dv16, ga, gb = _proj(x, tab_a, tab_b, wp)

    if past is None:
        ckv_all, kr_all, dk_all, dv_all = ckv, kr, dk16, dv16
        q_off = 0
        tq = min(256, s)
    else:
        ckv_p, kr_p, dk_p, dv_p = past
        past_len = ckv_p.shape[1]
        ckv_all = jnp.concatenate([ckv_p, ckv], axis=1)
        kr_all = jnp.concatenate([jnp.pad(kr_p, ((0, 0), (0, 0), (0, LANES - QK_ROPE))), kr], axis=1)
        dk_all = jnp.concatenate([dk_p.reshape(b, past_len, D_MODEL).astype(BF16), dk16], axis=1)
        dv_all = jnp.concatenate([dv_p.reshape(b, past_len, D_MODEL).astype(BF16), dv16], axis=1)
        q_off = past_len
        tq = s
    sk = ckv_all.shape[1]

    k_cat, v_a = _kv_up(ckv_all.reshape(b * sk, KV_LORA), kr_all.reshape(b * sk, LANES), wp)
    o_a = _attention(q_cat, k_cat.reshape(b, sk, D_MODEL), v_a.reshape(b, sk, D_MODEL), wp["lamv"], wp["g_diff_sub"],
                     maps=1, q_off=q_off, lam_init=lam_init, tq=tq)
    o_b = _attention(dq, dk_all, dv_all, wp["lamv"], wp["g_diff_sub"],
                     maps=2, q_off=q_off, lam_init=lam_init, tq=tq)

    n = b * s
    flat = lambda a: a.reshape(n, a.shape[-1])
    h, hn, route = _merge(flat(x), flat(o_a), flat(o_b), flat(ga), flat(gb), wp)

    n_tiles = (2 * n) // MOE_TILE + N_EXPERTS
    tile_expert, n_used, src_tok, slot_of = _routing_tables(route, n_tiles)
    y = _moe(hn, tile_expert, n_used, src_tok, wp)
    out = _combine(route, h, flat(pe), y, slot_of, wp)

    return (out.reshape(b, s, D_MODEL),
            (ckv, kr[:, :, :QK_ROPE], dk32.reshape(b, s, H_B, 2, D_B), dv32.reshape(b, s, H_B, V_B)))


def kernel(x_prompt, x_sample, p_prompt, p_sample, cache_mla_ckv, cache_mla_krope, cache_diff_k, cache_diff_v,
           g_mix, w_in, g_q_lat, w_uq, g_kv_lat, w_uk, w_uv, g_mla_qn, g_mla_qr, g_mla_kn, g_mla_kr,
           g_diff_q, g_diff_k, lambda_q1, lambda_k1, lambda_q2, lambda_k2, g_diff_sub, w_out,
           g_ffn, w_router_grp, b_router_grp, w_router_exp, b_router_exp, w_exp_in, w_exp_out,
           g_ple, w_ple_gate, w_ple_proj):
    depth = w_in.shape[0]
    pos_p = jnp.arange(x_prompt.shape[1], dtype=jnp.int32)
    pos_s = cache_mla_ckv.shape[2] + jnp.arange(x_sample.shape[1], dtype=jnp.int32)
    hp, hs = x_prompt, x_sample
    st_p, st_s = [], []
    for l in range(depth):
        wp = _prep_weights(l, g_mix, w_in, g_q_lat, w_uq, g_kv_lat, w_uk, w_uv, g_mla_qn, g_mla_qr, g_mla_kn,
                           g_mla_kr, g_diff_q, g_diff_k, lambda_q1, lambda_k1, lambda_q2, lambda_k2, g_diff_sub,
                           w_out, g_ffn, w_router_grp, b_router_grp, w_router_exp, b_router_exp, w_exp_in,
                           w_exp_out, g_ple, w_ple_gate, w_ple_proj)
        lam_init = 0.8 - 0.6 * math.exp(-0.3 * l)
        hp, sp = _layer(hp, p_prompt[l], pos_p, None, wp, lam_init)
        hs, ss = _layer(hs, p_sample[l], pos_s,
                        (cache_mla_ckv[l], cache_mla_krope[l], cache_diff_k[l], cache_diff_v[l]), wp, lam_init)
        st_p.append(sp)
        st_s.append(ss)
    stack = lambda sts, j: jnp.stack([st[j] for st in sts])
    return (hp, hs,
            stack(st_p, 0), stack(st_p, 1), stack(st_p, 2), stack(st_p, 3),
            stack(st_s, 0), stack(st_s, 1), stack(st_s, 2), stack(st_s, 3))
```

```python
import functools
import math

import jax
import jax.numpy as jnp
from jax import lax
from jax.experimental import pallas as pl
from jax.experimental.pallas import tpu as pltpu

F32 = jnp.float32
BF16 = jnp.bfloat16

D_MODEL = 1024
CHUNK = 64
ROPE_THETA = 500000.0
EPS = 1e-6
H_A = 8
Q_LORA = 384
KV_LORA = 256
QK_NOPE = 64
QK_ROPE = 32
V_A = 128
MLA_SCALE = 1.0 / math.sqrt(QK_NOPE + QK_ROPE)
H_B = 8
D_B = 64
V_B = 2 * D_B
ROT_B = D_B // 4
DIFF_SCALE = 1.0 / math.sqrt(D_B)
LOG2E = math.log2(math.e)
N_GROUPS = 4
EXP_PER_GROUP = 8
N_EXPERTS = N_GROUPS * EXP_PER_GROUP
D_EXPERT = 256
D_PLE = 256
IN_SIZES = (Q_LORA, KV_LORA, QK_ROPE, H_B * 2 * D_B, H_B * 2 * D_B, H_B * V_B, D_MODEL, D_MODEL)

LANES = 128
N_HEADS = 8
VMEM_LIMIT = 56 * 1024 * 1024

_O_Q = 0
_O_CKV = _O_Q + Q_LORA
_O_KR = _O_CKV + KV_LORA
_O_DQ = _O_KR + LANES
_O_DK = _O_DQ + D_MODEL
_O_DV = _O_DK + D_MODEL
_O_GA = _O_DV + D_MODEL
_O_GB = _O_GA + D_MODEL
_W_MAIN_COLS = _O_GB + D_MODEL


def _dot(a, b):
    return jnp.dot(a, b, preferred_element_type=F32)


def _sigmoid(x):
    return 1.0 / (1.0 + jnp.exp(-x))


def _row_rms(x, g):
    ms = jnp.mean(x * x, axis=-1, keepdims=True)
    return (x * lax.rsqrt(ms + EPS)) * g


def _seg_rms(z, bd, g):
    ms = _dot((z * z).astype(BF16), bd)
    return (z * lax.rsqrt(ms + EPS)) * g


def _rope_block(y, tab_ref, half):
    return (y * tab_ref[0]
            + pltpu.roll(y, LANES - half, 1) * tab_ref[1]
            + pltpu.roll(y, half, 1) * tab_ref[2])


def _proj_kernel(x_ref, taba_ref, tabb_ref, gmix_ref, w_ref, gql_ref, wuq_ref, gkv_ref, gv_ref, bd_ref,
                 qcat_ref, ckv_ref, kr_ref, dq_ref, dk32_ref, dk16_ref, dv32_ref, dv16_ref, ga_ref, gb_ref):
    xn = _row_rms(x_ref[...], gmix_ref[...]).astype(BF16)
    bd_a = bd_ref[0]
    bd_b = bd_ref[1]

    ql = _row_rms(_dot(xn, w_ref[:, _O_Q:_O_Q + Q_LORA]), gql_ref[...]).astype(BF16)
    q = _dot(ql, wuq_ref[...])
    for h in range(N_HEADS):
        sl = slice(h * LANES, (h + 1) * LANES)
        qb = _rope_block(_seg_rms(q[:, sl], bd_a, gv_ref[0:1, :]), taba_ref, QK_ROPE // 2)
        qcat_ref[:, sl] = (qb * (MLA_SCALE * LOG2E)).astype(BF16)

    ckv_ref[...] = _row_rms(_dot(xn, w_ref[:, _O_CKV:_O_CKV + KV_LORA]), gkv_ref[...])
    kr = _dot(xn, w_ref[:, _O_KR:_O_KR + LANES])
    kr_ref[...] = _rope_block(_seg_rms(kr, bd_a, gv_ref[1:2, :]), taba_ref, QK_ROPE // 2)

    zq = _dot(xn, w_ref[:, _O_DQ:_O_DQ + D_MODEL])
    zk = _dot(xn, w_ref[:, _O_DK:_O_DK + D_MODEL])
    for h in range(N_HEADS):
        sl = slice(h * LANES, (h + 1) * LANES)
        qb = _rope_block(_seg_rms(zq[:, sl], bd_b, gv_ref[2:3, :]), tabb_ref, ROT_B // 2)
        dq_ref[:, sl] = (qb * (DIFF_SCALE * LOG2E)).astype(BF16)
        kb = _rope_block(_seg_rms(zk[:, sl], bd_b, gv_ref[3:4, :]), tabb_ref, ROT_B // 2)
        dk32_ref[:, sl] = kb
        dk16_ref[:, sl] = kb.astype(BF16)

    dv = _dot(xn, w_ref[:, _O_DV:_O_DV + D_MODEL])
    dv32_ref[...] = dv
    dv16_ref[...] = dv.astype(BF16)
    ga_ref[...] = _sigmoid(_dot(xn, w_ref[:, _O_GA:_O_GA + D_MODEL])).astype(BF16)
    gb_ref[...] = _sigmoid(_dot(xn, w_ref[:, _O_GB:_O_GB + D_MODEL])).astype(BF16)


def _const_spec(shape):
    nd = len(shape)
    return pl.BlockSpec(shape, lambda *_: (0,) * nd, pipeline_mode=pl.Buffered(1))


def _proj(x, tab_a, tab_b, wp):
    b, s, _ = x.shape
    tm = min(512, s)
    grid = (b, s // tm)

    def tok(width):
        return pl.BlockSpec((None, tm, width), lambda bi, i: (bi, i, 0))

    tab_spec = pl.BlockSpec((3, tm, LANES), lambda bi, i: (0, i, 0))
    out_shapes = [
        jax.ShapeDtypeStruct((b, s, D_MODEL), BF16),
        jax.ShapeDtypeStruct((b, s, KV_LORA), F32),
        jax.ShapeDtypeStruct((b, s, LANES), F32),
        jax.ShapeDtypeStruct((b, s, D_MODEL), BF16),
        jax.ShapeDtypeStruct((b, s, D_MODEL), F32),
        jax.ShapeDtypeStruct((b, s, D_MODEL), BF16),
        jax.ShapeDtypeStruct((b, s, D_MODEL), F32),
        jax.ShapeDtypeStruct((b, s, D_MODEL), BF16),
        jax.ShapeDtypeStruct((b, s, D_MODEL), BF16),
        jax.ShapeDtypeStruct((b, s, D_MODEL), BF16),
    ]
    out_specs = [tok(D_MODEL), tok(KV_LORA), tok(LANES), tok(D_MODEL), tok(D_MODEL), tok(D_MODEL),
                 tok(D_MODEL), tok(D_MODEL), tok(D_MODEL), tok(D_MODEL)]
    in_specs = [
        tok(D_MODEL), tab_spec, tab_spec,
        _const_spec((1, D_MODEL)), _const_spec((D_MODEL, _W_MAIN_COLS)),
        _const_spec((1, Q_LORA)), _const_spec((Q_LORA, D_MODEL)), _const_spec((1, KV_LORA)),
        _const_spec((8, LANES)), _const_spec((2, LANES, LANES)),
    ]
    return pl.pallas_call(
        _proj_kernel, grid=grid, in_specs=in_specs, out_specs=out_specs, out_shape=out_shapes,
        compiler_params=pltpu.CompilerParams(dimension_semantics=("parallel", "parallel"),
                                             vmem_limit_bytes=VMEM_LIMIT),
        name="proj",
    )(x, tab_a, tab_b, wp["g_mix"], wp["w_main"], wp["g_q_lat"], wp["w_uq"], wp["g_kv_lat"],
      wp["gvecs"], wp["bd"])


def _kvup_kernel(ckv_ref, kr_ref, wuk_ref, wuv_ref, gv_ref, bd_ref, kcat_ref, va_ref):
    c = ckv_ref[...].astype(BF16)
    kn = _dot(c, wuk_ref[...])
    kr = kr_ref[...]
    bd_a = bd_ref[0]
    for h in range(N_HEADS):
        sl = slice(h * LANES, (h + 1) * LANES)
        kcat_ref[:, sl] = (_seg_rms(kn[:, sl], bd_a, gv_ref[4:5, :]) + kr).astype(BF16)
    va_ref[...] = _dot(c, wuv_ref[...]).astype(BF16)


def _kv_up(ckv, kr, wp):
    m = ckv.shape[0]
    tm = 256
    assert m % tm == 0

    def tok(width):
        return pl.BlockSpec((tm, width), lambda i: (i, 0))

    return pl.pallas_call(
        _kvup_kernel, grid=(m // tm,),
        in_specs=[tok(KV_LORA), tok(LANES), _const_spec((KV_LORA, D_MODEL)), _const_spec((KV_LORA, D_MODEL)),
                  _const_spec((8, LANES)), _const_spec((2, LANES, LANES))],
        out_specs=[tok(D_MODEL), tok(D_MODEL)],
        out_shape=[jax.ShapeDtypeStruct((m, D_MODEL), BF16), jax.ShapeDtypeStruct((m, D_MODEL), BF16)],
        compiler_params=pltpu.CompilerParams(dimension_semantics=("parallel",), vmem_limit_bytes=VMEM_LIMIT),
        name="kv_up",
    )(ckv, kr, wp["w_uk"], wp["w_uv"], wp["gvecs"], wp["bd"])


def _attn_kernel(lamv_ref, gsub_ref, q_ref, k_ref, v_ref, o_ref, qt_sc, m_sc, l_sc, acc_sc, st0_sc, st1_sc,
                 *, maps, tq, q_off, lam_init):
    qi = pl.program_id(1)
    rows = maps * tq
    n_full = q_off // tq + qi

    eye = (lax.broadcasted_iota(jnp.int32, (LANES, LANES), 0)
           == lax.broadcasted_iota(jnp.int32, (LANES, LANES), 1)).astype(BF16)
    for h in range(N_HEADS):
        sl = slice(h * LANES, (h + 1) * LANES)
        qt = lax.dot_general(eye, q_ref[:, sl], (((1,), (1,)), ((), ())), preferred_element_type=F32).astype(BF16)
        if maps == 2:
            sub = lax.broadcasted_iota(jnp.int32, (LANES, tq), 0)
            zero = jnp.zeros_like(qt)
            qt = jnp.concatenate([jnp.where(sub < D_B, qt, zero), jnp.where(sub >= D_B, qt, zero)], axis=1)
        qt_sc[h] = qt
        m_sc[h] = jnp.full((1, rows), -jnp.inf, F32)
        l_sc[h] = jnp.zeros((1, rows), F32)
        acc_sc[h] = jnp.zeros((LANES, rows), F32)

    def scores(h, koff, buf):
        buf[h] = _dot(k_ref[pl.ds(koff, tq), h * LANES:(h + 1) * LANES], qt_sc[h])

    def softmax_pv(h, koff, buf, vis):
        sl = slice(h * LANES, (h + 1) * LANES)
        st = buf[h]
        if vis is not None:
            st = jnp.where(vis, st, -jnp.inf)
        m_prev = m_sc[h]
        m_next = jnp.maximum(m_prev, jnp.max(st, axis=0, keepdims=True))
        alpha = jnp.exp2(m_prev - m_next)
        pt = jnp.exp2(st - m_next)
        l_sc[h] = alpha * l_sc[h] + jnp.sum(pt, axis=0, keepdims=True)
        pv = lax.dot_general(v_ref[pl.ds(koff, tq), sl], pt.astype(BF16), (((0,), (0,)), ((), ())),
                             preferred_element_type=F32)
        acc_sc[h] = alpha * acc_sc[h] + pv
        m_sc[h] = m_next

    for h in range(N_HEADS):
        scores(h, 0, st0_sc)

    def step(c, cur, nxt):
        koff = pl.multiple_of(c * tq, tq)
        knext = pl.multiple_of((c + 1) * tq, tq)
        scores(0, knext, nxt)
        for h in range(N_HEADS):
            if h + 1 < N_HEADS:
                scores(h + 1, knext, nxt)
            softmax_pv(h, koff, cur, None)

    def chunk_pair(i, carry):
        step(2 * i, st0_sc, st1_sc)
        step(2 * i + 1, st1_sc, st0_sc)
        return carry

    lax.fori_loop(0, n_full // 2, chunk_pair, 0)
    odd = n_full % 2 == 1

    @pl.when(odd)
    def _odd_tail():
        step(n_full - 1, st0_sc, st1_sc)

    kc = lax.broadcasted_iota(jnp.int32, (tq, rows), 0) >> 6
    qc = lax.broadcasted_iota(jnp.int32, (tq, rows), 1)
    if maps == 2:
        qc = jnp.where(qc >= tq, qc - tq, qc)
    vis = kc <= (qc >> 6)
    kdiag = pl.multiple_of(n_full * tq, tq)

    @pl.when(odd)
    def _diag_odd():
        for h in range(N_HEADS):
            softmax_pv(h, kdiag, st1_sc, vis)

    @pl.when(jnp.logical_not(odd))
    def _diag_even():
        for h in range(N_HEADS):
            softmax_pv(h, kdiag, st0_sc, vis)

    if maps == 2:
        lv = lamv_ref[...]
        lam = (jnp.exp(jnp.sum(lv[0:1] * lv[1:2], axis=-1, keepdims=True))
               - jnp.exp(jnp.sum(lv[2:3] * lv[3:4], axis=-1, keepdims=True)) + lam_init)
    eye_q = (lax.broadcasted_iota(jnp.int32, (tq, tq), 0)
             == lax.broadcasted_iota(jnp.int32, (tq, tq), 1)).astype(BF16)
    for h in range(N_HEADS):
        sl = slice(h * LANES, (h + 1) * LANES)
        ot = acc_sc[h] / l_sc[h]
        if maps == 2:
            ot = ot[:, :tq] - lam * ot[:, tq:]
            ms = jnp.mean(ot * ot, axis=0, keepdims=True)
            ot = ((ot * lax.rsqrt(ms + EPS)) * gsub_ref[...]) * (1.0 - lam_init)
        o = lax.dot_general(eye_q, ot.astype(BF16), (((1,), (1,)), ((), ())), preferred_element_type=F32)
        o_ref[:, sl] = o.astype(o_ref.dtype)


def _attention(q, k, v, lamv, gsub, *, maps, q_off, lam_init, tq):
    b, sq, _ = q.shape
    sk = k.shape[1]
    assert sq % tq == 0 and q_off % tq == 0 and tq % CHUNK == 0 and sk == q_off + sq
    rows = maps * tq
    kern = functools.partial(_attn_kernel, maps=maps, tq=tq, q_off=q_off, lam_init=lam_init)
    kv_spec = pl.BlockSpec((None, sk, D_MODEL), lambda bi, qi: (bi, 0, 0), pipeline_mode=pl.Buffered(1))
    return pl.pallas_call(
        kern, grid=(b, sq // tq),
        in_specs=[_const_spec((4, LANES)), _const_spec((LANES, 1)),
                  pl.BlockSpec((None, tq, D_MODEL), lambda bi, qi: (bi, qi, 0)), kv_spec, kv_spec],
        out_specs=pl.BlockSpec((None, tq, D_MODEL), lambda bi, qi: (bi, qi, 0)),
        out_shape=jax.ShapeDtypeStruct((b, sq, D_MODEL), BF16),
        scratch_shapes=[pltpu.VMEM((N_HEADS, LANES, rows), BF16), pltpu.VMEM((N_HEADS, 1, rows), F32),
                        pltpu.VMEM((N_HEADS, 1, rows), F32), pltpu.VMEM((N_HEADS, LANES, rows), F32),
                        pltpu.VMEM((N_HEADS, tq, rows), F32), pltpu.VMEM((N_HEADS, tq, rows), F32)],
        compiler_params=pltpu.CompilerParams(dimension_semantics=("parallel", "arbitrary"),
                                             vmem_limit_bytes=VMEM_LIMIT),
        name="attn_diff" if maps == 2 else "attn_mla",
    )(lamv, gsub, q, k, v)


def _merge_kernel(x_ref, oa_ref, ob_ref, ga_ref, gb_ref, wout_ref, gffn_ref, wrh_ref, wrl_ref, br_ref,
                  h_ref, hn_ref, route_ref, counts_ref, cnt_sc):
    merged = (ga_ref[...].astype(F32) * oa_ref[...].astype(F32)
              + gb_ref[...].astype(F32) * ob_ref[...].astype(F32))
    h = x_ref[...] + _dot(merged.astype(BF16), wout_ref[...])
    h_ref[...] = h
    hn = _row_rms(h, gffn_ref[...])
    hn_ref[...] = hn

    hi = hn.astype(BF16)
    lo = (hn - hi.astype(F32)).astype(BF16)
    logits = _dot(hi, wrh_ref[...]) + (_dot(lo, wrh_ref[...]) + _dot(hi, wrl_ref[...])) + br_ref[...]
    lane = lax.broadcasted_iota(jnp.int32, logits.shape, 1).astype(F32)
    neg = jnp.full_like(logits, -jnp.inf)
    big = jnp.full_like(logits, 1e9)

    def first_argmax(vals):
        vmax = jnp.max(vals, axis=1, keepdims=True)
        return vmax, jnp.min(jnp.where(vals == vmax, lane, big), axis=1, keepdims=True)

    gmask = lane < N_GROUPS
    gmax, gsel = first_argmax(jnp.where(gmask, logits, neg))
    p_top = 1.0 / jnp.sum(jnp.where(gmask, jnp.exp(logits - gmax), 0.0), axis=1, keepdims=True)
    e_lo = N_GROUPS + gsel * EXP_PER_GROUP
    emask = jnp.logical_and(lane >= e_lo, lane < e_lo + EXP_PER_GROUP)
    le = jnp.where(emask, logits, neg)
    v1, i1 = first_argmax(le)
    v2, i2 = first_argmax(jnp.where(lane == i1, neg, le))
    t = jnp.exp(v2 - v1)
    w1 = p_top / (1.0 + t)
    w2 = p_top * t / (1.0 + t)
    e1 = i1 - N_GROUPS
    e2 = i2 - N_GROUPS

    @pl.when(pl.program_id(0) == 0)
    def _zero_counts():
        cnt_sc[...] = jnp.zeros(cnt_sc.shape, F32)

    tm = logits.shape[0]
    oh1 = lane == e1
    oh2 = lane == e2
    earlier = (lax.broadcasted_iota(jnp.int32, (tm, tm), 1)
               < lax.broadcasted_iota(jnp.int32, (tm, tm), 0)).astype(BF16)
    before1 = _dot(earlier, oh1.astype(BF16))
    before2 = _dot(earlier, oh2.astype(BF16))
    base = cnt_sc[...]
    c1 = jnp.sum(oh1.astype(F32), axis=0, keepdims=True)
    c2 = jnp.sum(oh2.astype(F32), axis=0, keepdims=True)
    rank1 = jnp.sum(jnp.where(oh1, base + before1, 0.0), axis=1, keepdims=True)
    rank2 = jnp.sum(jnp.where(oh2, (base + c1) + before2, 0.0), axis=1, keepdims=True)
    total = base + (c1 + c2)
    cnt_sc[...] = total
    counts_ref[...] = total

    route = jnp.where(lane == 0, e1, 0.0)
    route = jnp.where(lane == 1, e2, route)
    route = jnp.where(lane == 2, w1, route)
    route = jnp.where(lane == 3, w2, route)
    route = jnp.where(lane == 4, rank1, route)
    route = jnp.where(lane == 5, rank2, route)
    route_ref[...] = route


def _merge(x, oa, ob, ga, gb, wp):
    n = x.shape[0]
    tm = 256
    assert n % tm == 0

    def tok(width):
        return pl.BlockSpec((tm, width), lambda i: (i, 0))

    return pl.pallas_call(
        _merge_kernel, grid=(n // tm,),
        in_specs=[tok(D_MODEL)] * 5 + [_const_spec((D_MODEL, D_MODEL)), _const_spec((1, D_MODEL)),
                                       _const_spec((D_MODEL, LANES)), _const_spec((D_MODEL, LANES)),
                                       _const_spec((1, LANES))],
        out_specs=[tok(D_MODEL), tok(D_MODEL), tok(LANES), pl.BlockSpec((1, LANES), lambda i: (0, 0))],
        out_shape=[jax.ShapeDtypeStruct((n, D_MODEL), F32), jax.ShapeDtypeStruct((n, D_MODEL), F32),
                   jax.ShapeDtypeStruct((n, LANES), F32), jax.ShapeDtypeStruct((1, LANES), F32)],
        scratch_shapes=[pltpu.VMEM((1, LANES), F32)],
        compiler_params=pltpu.CompilerParams(dimension_semantics=("arbitrary",), vmem_limit_bytes=VMEM_LIMIT),
        name="merge",
    )(x, oa, ob, ga, gb, wp["w_out"], wp["g_ffn"], wp["w_r_hi"], wp["w_r_lo"], wp["b_r"])


MOE_TILE = 256


def _gather_rows(idx_ref, n_rows, src_hbm, dst_ref, sem, *, unrolled):
    def start(r):
        pltpu.make_async_copy(src_hbm.at[pl.ds(idx_ref[0, 0, r], 1), :], dst_ref.at[pl.ds(r, 1), :], sem).start()

    if unrolled:
        for r in range(n_rows):
            start(r)
    else:
        def body(r, carry):
            start(r)
            return carry
        lax.fori_loop(0, n_rows, body, 0, unroll=8)


def _wait_rows(n_rows, src_hbm, dst_ref, sem):
    pltpu.make_async_copy(src_hbm.at[pl.ds(0, n_rows), :], dst_ref, sem).wait()


DISPATCH_TILE = 256


DISPATCH_SLOTS = 3


def _dispatch_kernel(pos_ref, hn_hbm, xs_hbm, buf, load_sem, scat_sem):
    i = pl.program_id(0)
    n = pl.num_programs(0)
    td = DISPATCH_TILE
    slot = i % DISPATCH_SLOTS

    def load(t, s):
        return pltpu.make_async_copy(hn_hbm.at[pl.ds(pl.multiple_of(t * td, td), td), :], buf.at[s], load_sem.at[s])

    def drain(s):
        pltpu.make_async_copy(xs_hbm.at[pl.ds(0, 2 * td), :], xs_hbm.at[pl.ds(0, 2 * td), :], scat_sem.at[s]).wait()

    @pl.when(i == 0)
    def _first_load():
        load(0, 0).start()

    @pl.when(i + 1 < n)
    def _next_load():
        nxt = (i + 1) % DISPATCH_SLOTS

        @pl.when(i >= 2)
        def _free_slot():
            drain(nxt)

        load(i + 1, nxt).start()

    load(i, slot).wait()
    for r in range(td):
        for c in range(2):
            dst = xs_hbm.at[pl.ds(pos_ref[0, 0, c * td + r], 1), :]
            pltpu.make_async_copy(buf.at[slot, pl.ds(r, 1), :], dst, scat_sem.at[slot]).start()

    @pl.when(i == n - 1)
    def _drain_all():
        @pl.when(i >= 2)
        def _():
            drain((i + 1) % DISPATCH_SLOTS)

        @pl.when(i >= 1)
        def _():
            drain((i + 2) % DISPATCH_SLOTS)

        drain(slot)


def _dispatch(hn, pos3):
    n = hn.shape[0]
    assert n % DISPATCH_TILE == 0
    return pl.pallas_call(
        _dispatch_kernel, grid=(n // DISPATCH_TILE,),
        in_specs=[pl.BlockSpec((1, 1, 2 * DISPATCH_TILE), lambda i: (i, 0, 0), memory_space=pltpu.SMEM),
                  pl.BlockSpec(memory_space=pl.ANY)],
        out_specs=pl.BlockSpec(memory_space=pl.ANY),
        out_shape=jax.ShapeDtypeStruct((2 * n, D_MODEL), F32),
        scratch_shapes=[pltpu.VMEM((DISPATCH_SLOTS, DISPATCH_TILE, D_MODEL), F32),
                        pltpu.SemaphoreType.DMA((DISPATCH_SLOTS,)), pltpu.SemaphoreType.DMA((DISPATCH_SLOTS,))],
        compiler_params=pltpu.CompilerParams(dimension_semantics=("arbitrary",)),
        name="dispatch",
    )(pos3, hn)


def _moe_kernel(vtile_ref, vexp_ref, vlo_ref, vhi_ref, vfirst_ref, xs_ref, wei_ref, weo_ref, y_ref):
    del vexp_ref
    v = pl.program_id(0)
    x = xs_ref[...].astype(BF16)
    ab = _dot(x, wei_ref[...])
    a = ab[:, :D_EXPERT]
    act = (a * _sigmoid(a)) * ab[:, D_EXPERT:]
    y = _dot(act.astype(BF16), weo_ref[...])

    @pl.when(vfirst_ref[v] == 1)
    def _first_visit():
        y_ref[...] = y

    @pl.when(vfirst_ref[v] == 0)
    def _later_visit():
        row = vtile_ref[v] * MOE_TILE + lax.broadcasted_iota(jnp.int32, (MOE_TILE, 1), 0)
        mine = jnp.logical_and(row >= vlo_ref[v], row < vhi_ref[v])
        y_ref[...] = jnp.where(mine, y, y_ref[...])


def _moe(xs, visits, wp):
    n_visits = visits[0].shape[0]
    grid_spec = pltpu.PrefetchScalarGridSpec(
        num_scalar_prefetch=5, grid=(n_visits,),
        in_specs=[pl.BlockSpec((MOE_TILE, D_MODEL), lambda v, vt, ve, lo, hi, fi: (vt[v], 0)),
                  pl.BlockSpec((None, D_MODEL, 2 * D_EXPERT), lambda v, vt, ve, lo, hi, fi: (ve[v], 0, 0)),
                  pl.BlockSpec((None, D_EXPERT, D_MODEL), lambda v, vt, ve, lo, hi, fi: (ve[v], 0, 0))],
        out_specs=pl.BlockSpec((MOE_TILE, D_MODEL), lambda v, vt, ve, lo, hi, fi: (vt[v], 0)),
    )
    return pl.pallas_call(
        _moe_kernel, grid_spec=grid_spec,
        out_shape=jax.ShapeDtypeStruct(xs.shape, F32),
        compiler_params=pltpu.CompilerParams(dimension_semantics=("arbitrary",), vmem_limit_bytes=VMEM_LIMIT),
        name="moe",
    )(*visits, xs, wp["w_exp_in"], wp["w_exp_out"])


def _routing_tables(route, counts, n_visits):
    e = route[:, 0:2].astype(jnp.int32)
    rank = route[:, 4:6].astype(jnp.int32)
    cnt = counts[0, :N_EXPERTS].astype(jnp.int32)
    end = jnp.cumsum(cnt)
    start = end - cnt
    ids = jnp.arange(N_EXPERTS, dtype=jnp.int32)
    pos = rank + jnp.sum(jnp.where(e[:, :, None] == ids[None, None, :], start[None, None, :], 0), axis=2)

    first_tile = start // MOE_TILE
    n_vis = jnp.where(cnt > 0, (end - 1) // MOE_TILE - first_tile + 1, 0)
    vis_end = jnp.cumsum(n_vis)
    vis_start = vis_end - n_vis
    v = jnp.minimum(jnp.arange(n_visits, dtype=jnp.int32), vis_end[-1] - 1)
    v_exp = jnp.sum((v[:, None] >= vis_end[None, :]).astype(jnp.int32), axis=1)
    pick = lambda table: jnp.sum(jnp.where(v_exp[:, None] == ids[None, :], table[None, :], 0), axis=1)
    v_tile = pick(first_tile) + (v - pick(vis_start))
    v_first = jnp.concatenate([jnp.ones((1,), jnp.int32), (v_tile[1:] != v_tile[:-1]).astype(jnp.int32)])
    visits = tuple(a.astype(jnp.int32) for a in (v_tile, v_exp, pick(start), pick(end), v_first))
    return pos, visits


COMB_TILE = 256


def _combine_kernel(pos_ref, posn_ref, route_ref, h_ref, pe_ref, y_hbm, gple_ref, wg_ref, wp_ref,
                    o_ref, ybuf, sem):
    i = pl.program_id(0)
    n = pl.num_programs(0)
    slot = i % 2
    rows = 2 * COMB_TILE

    @pl.when(i == 0)
    def _first():
        _gather_rows(pos_ref, rows, y_hbm, ybuf.at[0], sem.at[0], unrolled=False)

    @pl.when(i + 1 < n)
    def _prefetch():
        _gather_rows(posn_ref, rows, y_hbm, ybuf.at[1 - slot], sem.at[1 - slot], unrolled=True)

    _wait_rows(rows, y_hbm, ybuf.at[slot], sem.at[slot])
    route = route_ref[...]
    h = h_ref[...] + (route[:, 2:3] * ybuf[slot, 0:COMB_TILE, :] + route[:, 3:4] * ybuf[slot, COMB_TILE:rows, :])
    gate = _sigmoid(_dot(_row_rms(h, gple_ref[...]).astype(BF16), wg_ref[...]))
    o_ref[...] = h + gate * _dot(pe_ref[...].astype(BF16), wp_ref[...])


def _combine(route, h, pe, y, pos3, wp):
    n = h.shape[0]
    assert n % COMB_TILE == 0 and COMB_TILE == DISPATCH_TILE
    n_tiles = n // COMB_TILE

    def tok(width):
        return pl.BlockSpec((COMB_TILE, width), lambda i: (i, 0))

    pos_spec = pl.BlockSpec((1, 1, 2 * COMB_TILE), lambda i: (i, 0, 0), memory_space=pltpu.SMEM)
    posn_spec = pl.BlockSpec((1, 1, 2 * COMB_TILE), lambda i: (jnp.minimum(i + 1, n_tiles - 1), 0, 0),
                             memory_space=pltpu.SMEM)
    return pl.pallas_call(
        _combine_kernel, grid=(n_tiles,),
        in_specs=[pos_spec, posn_spec, tok(LANES), tok(D_MODEL), tok(D_PLE), pl.BlockSpec(memory_space=pl.ANY),
                  _const_spec((1, D_MODEL)), _const_spec((D_MODEL, D_MODEL)), _const_spec((D_PLE, D_MODEL))],
        out_specs=tok(D_MODEL),
        out_shape=jax.ShapeDtypeStruct((n, D_MODEL), F32),
        scratch_shapes=[pltpu.VMEM((2, 2 * COMB_TILE, D_MODEL), F32), pltpu.SemaphoreType.DMA((2,))],
        compiler_params=pltpu.CompilerParams(dimension_semantics=("arbitrary",), vmem_limit_bytes=VMEM_LIMIT),
        name="combine",
    )(pos3, pos3, route, h, pe, y, wp["g_ple"], wp["w_ple_gate"], wp["w_ple_proj"])


def _rope_tables(pos, rot, offsets):
    half = rot // 2
    inv = ROPE_THETA ** (-jnp.arange(half, dtype=F32) * (2.0 / rot))
    lane = jnp.arange(LANES)
    rel = jnp.full((LANES,), -1)
    for o in offsets:
        rel = jnp.where((lane >= o) & (lane < o + rot), lane - o, rel)
    first, second = (rel >= 0) & (rel < half), rel >= half
    ang = pos.astype(F32)[:, None] * inv[jnp.maximum(rel, 0) % half][None, :]
    c, s = jnp.cos(ang), jnp.sin(ang)
    return jnp.stack([jnp.where(rel >= 0, c, 1.0), jnp.where(first, -s, 0.0), jnp.where(second, s, 0.0)])


def _prep_weights(l, g_mix, w_in, g_q_lat, w_uq, g_kv_lat, w_uk, w_uv, g_mla_qn, g_mla_qr, g_mla_kn, g_mla_kr,
                  g_diff_q, g_diff_k, lambda_q1, lambda_k1, lambda_q2, lambda_k2, g_diff_sub, w_out,
                  g_ffn, w_router_grp, b_router_grp, w_router_exp, b_router_exp, w_exp_in, w_exp_out,
                  g_ple, w_ple_gate, w_ple_proj):
    offs = [0]
    for sz in IN_SIZES:
        offs.append(offs[-1] + sz)
    wi = w_in[l]
    seg = [wi[:, offs[j]:offs[j + 1]] for j in range(len(IN_SIZES))]
    w_kr_pad = jnp.pad(seg[2], ((0, 0), (0, LANES - QK_ROPE)))
    w_main = jnp.concatenate([seg[0], seg[1], w_kr_pad] + seg[3:], axis=1).astype(BF16)

    hd = QK_NOPE + QK_ROPE
    uq = w_uq[l].reshape(Q_LORA, H_A, hd)
    uq_pad = jnp.concatenate([uq[:, :, QK_NOPE:], jnp.zeros((Q_LORA, H_A, LANES - hd), F32), uq[:, :, :QK_NOPE]],
                             axis=2).reshape(Q_LORA, H_A * LANES).astype(BF16)
    uk = w_uk[l].reshape(KV_LORA, H_A, QK_NOPE)
    uk_pad = jnp.concatenate([jnp.zeros((KV_LORA, H_A, LANES - QK_NOPE), F32), uk], axis=2)
    uk_pad = uk_pad.reshape(KV_LORA, H_A * LANES).astype(BF16)

    z32 = jnp.zeros((LANES - hd,), F32)
    gvecs = jnp.stack([
        jnp.concatenate([g_mla_qr[l], z32, g_mla_qn[l]]),
        jnp.concatenate([g_mla_kr[l], jnp.zeros((LANES - QK_ROPE,), F32)]),
        jnp.concatenate([g_diff_q[l], g_diff_q[l]]),
        jnp.concatenate([g_diff_k[l], g_diff_k[l]]),
        jnp.concatenate([jnp.zeros((LANES - QK_NOPE,), F32), g_mla_kn[l]]),
        jnp.zeros((LANES,), F32), jnp.zeros((LANES,), F32), jnp.zeros((LANES,), F32)])

    lane = jnp.arange(LANES)
    seg_a = jnp.where(lane < QK_ROPE, 0, jnp.where(lane < LANES - QK_NOPE, -1, 1))
    same_a = (seg_a[:, None] == seg_a[None, :]) & (seg_a[:, None] >= 0)
    bd_a = jnp.where(same_a, jnp.where(seg_a[:, None] == 0, 1.0 / QK_ROPE, 1.0 / QK_NOPE), 0.0)
    seg_b = lane // D_B
    bd_b = jnp.where(seg_b[:, None] == seg_b[None, :], 1.0 / D_B, 0.0)
    bd = jnp.stack([bd_a, bd_b]).astype(BF16)

    w_r = jnp.concatenate([w_router_grp[l], w_router_exp[l],
                           jnp.zeros((D_MODEL, LANES - N_GROUPS - N_EXPERTS), F32)], axis=1)
    w_r_hi = w_r.astype(BF16)
    w_r_lo = (w_r - w_r_hi.astype(F32)).astype(BF16)
    b_r = jnp.concatenate([b_router_grp[l], b_router_exp[l],
                           jnp.zeros((LANES - N_GROUPS - N_EXPERTS,), F32)]).reshape(1, LANES)

    pad64 = jnp.zeros((LANES - D_B,), F32)
    lamv = jnp.stack([jnp.concatenate([v[l], pad64]) for v in (lambda_q1, lambda_k1, lambda_q2, lambda_k2)])

    return {
        "g_mix": g_mix[l].reshape(1, D_MODEL), "w_main": w_main,
        "g_q_lat": g_q_lat[l].reshape(1, Q_LORA), "w_uq": uq_pad,
        "g_kv_lat": g_kv_lat[l].reshape(1, KV_LORA), "w_uk": uk_pad, "w_uv": w_uv[l].astype(BF16),
        "gvecs": gvecs, "bd": bd, "lamv": lamv, "g_diff_sub": g_diff_sub[l].reshape(V_B, 1),
        "w_out": w_out[l].astype(BF16), "g_ffn": g_ffn[l].reshape(1, D_MODEL),
        "w_r_hi": w_r_hi, "w_r_lo": w_r_lo, "b_r": b_r,
        "w_exp_in": w_exp_in[l].astype(BF16), "w_exp_out": w_exp_out[l].astype(BF16),
        "g_ple": g_ple[l].reshape(1, D_MODEL), "w_ple_gate": w_ple_gate[l].astype(BF16),
        "w_ple_proj": w_ple_proj[l].astype(BF16),
    }


def _layer(x, pe, pos, past, wp, lam_init):
    b, s, _ = x.shape
    tab_a = _rope_tables(pos, QK_ROPE, (0,))
    tab_b = _rope_tables(pos, ROT_B, (0, D_B))
    q_cat, ckv, kr, dq, dk32, dk16, dv32, dv16, ga, gb = _proj(x, tab_a, tab_b, wp)

    if past is None:
        ckv_all, kr_all, dk_all, dv_all = ckv, kr, dk16, dv16
        q_off = 0
        tq = min(256, s)
    else:
        ckv_p, kr_p, dk_p, dv_p = past
        past_len = ckv_p.shape[1]
        ckv_all = jnp.concatenate([ckv_p, ckv], axis=1)
        kr_all = jnp.concatenate([jnp.pad(kr_p, ((0, 0), (0, 0), (0, LANES - QK_ROPE))), kr], axis=1)
        dk_all = jnp.concatenate([dk_p.reshape(b, past_len, D_MODEL).astype(BF16), dk16], axis=1)
        dv_all = jnp.concatenate([dv_p.reshape(b, past_len, D_MODEL).astype(BF16), dv16], axis=1)
        q_off = past_len
        tq = s
    sk = ckv_all.shape[1]

    k_cat, v_a = _kv_up(ckv_all.reshape(b * sk, KV_LORA), kr_all.reshape(b * sk, LANES), wp)
    o_a = _attention(q_cat, k_cat.reshape(b, sk, D_MODEL), v_a.reshape(b, sk, D_MODEL), wp["lamv"], wp["g_diff_sub"],
                     maps=1, q_off=q_off, lam_init=lam_init, tq=tq)
    o_b = _attention(dq, dk_all, dv_all, wp["lamv"], wp["g_diff_sub"],
                     maps=2, q_off=q_off, lam_init=lam_init, tq=tq)

    n = b * s
    flat = lambda a: a.reshape(n, a.shape[-1])
    h, hn, route, counts = _merge(flat(x), flat(o_a), flat(o_b), flat(ga), flat(gb), wp)

    slot_of, visits = _routing_tables(route, counts, (2 * n) // MOE_TILE + N_EXPERTS - 1)
    slots3 = slot_of.reshape(n // COMB_TILE, COMB_TILE, 2).transpose(0, 2, 1).reshape(n // COMB_TILE, 1, 2 * COMB_TILE)
    y = _moe(_dispatch(hn, slots3), visits, wp)
    out = _combine(route, h, flat(pe), y, slots3, wp)

    return (out.reshape(b, s, D_MODEL),
            (ckv, kr[:, :, :QK_ROPE], dk32.reshape(b, s, H_B, 2, D_B), dv32.reshape(b, s, H_B, V_B)))


def kernel(x_prompt, x_sample, p_prompt, p_sample, cache_mla_ckv, cache_mla_krope, cache_diff_k, cache_diff_v,
           g_mix, w_in, g_q_lat, w_uq, g_kv_lat, w_uk, w_uv, g_mla_qn, g_mla_qr, g_mla_kn, g_mla_kr,
           g_diff_q, g_diff_k, lambda_q1, lambda_k1, lambda_q2, lambda_k2, g_diff_sub, w_out,
           g_ffn, w_router_grp, b_router_grp, w_router_exp, b_router_exp, w_exp_in, w_exp_out,
           g_ple, w_ple_gate, w_ple_proj):
    depth = w_in.shape[0]
    pos_p = jnp.arange(x_prompt.shape[1], dtype=jnp.int32)
    pos_s = cache_mla_ckv.shape[2] + jnp.arange(x_sample.shape[1], dtype=jnp.int32)
    hp, hs = x_prompt, x_sample
    st_p, st_s = [], []
    for l in range(depth):
        wp = _prep_weights(l, g_mix, w_in, g_q_lat, w_uq, g_kv_lat, w_uk, w_uv, g_mla_qn, g_mla_qr, g_mla_kn,
                           g_mla_kr, g_diff_q, g_diff_k, lambda_q1, lambda_k1, lambda_q2, lambda_k2, g_diff_sub,
                           w_out, g_ffn, w_router_grp, b_router_grp, w_router_exp, b_router_exp, w_exp_in,
                           w_exp_out, g_ple, w_ple_gate, w_ple_proj)
        lam_init = 0.8 - 0.6 * math.exp(-0.3 * l)
        hp, sp = _layer(hp, p_prompt[l], pos_p, None, wp, lam_init)
        hs, ss = _layer(hs, p_sample[l], pos_s,
                        (cache_mla_ckv[l], cache_mla_krope[l], cache_diff_k[l], cache_diff_v[l]), wp, lam_init)
        st_p.append(sp)
        st_s.append(ss)
    stack = lambda sts, j: jnp.stack([st[j] for st in sts])
    return (hp, hs,
            stack(st_p, 0), stack(st_p, 1), stack(st_p, 2), stack(st_p, 3),
            stack(st_s, 0), stack(st_s, 1), stack(st_s, 2), stack(st_s, 3))
```

```python
import functools
import math

import jax
import jax.numpy as jnp
from jax import lax
from jax.experimental import pallas as pl
from jax.experimental.pallas import tpu as pltpu

F32 = jnp.float32
BF16 = jnp.bfloat16

D_MODEL = 1024
CHUNK = 64
ROPE_THETA = 500000.0
EPS = 1e-6
H_A = 8
Q_LORA = 384
KV_LORA = 256
QK_NOPE = 64
QK_ROPE = 32
V_A = 128
MLA_SCALE = 1.0 / math.sqrt(QK_NOPE + QK_ROPE)
H_B = 8
D_B = 64
V_B = 2 * D_B
ROT_B = D_B // 4
DIFF_SCALE = 1.0 / math.sqrt(D_B)
LOG2E = math.log2(math.e)
N_GROUPS = 4
EXP_PER_GROUP = 8
N_EXPERTS = N_GROUPS * EXP_PER_GROUP
D_EXPERT = 256
D_PLE = 256
IN_SIZES = (Q_LORA, KV_LORA, QK_ROPE, H_B * 2 * D_B, H_B * 2 * D_B, H_B * V_B, D_MODEL, D_MODEL)

LANES = 128
N_HEADS = 8
VMEM_LIMIT = 56 * 1024 * 1024

_O_Q = 0
_O_CKV = _O_Q + Q_LORA
_O_KR = _O_CKV + KV_LORA
_O_DQ = _O_KR + LANES
_O_DK = _O_DQ + D_MODEL
_O_DV = _O_DK + D_MODEL
_O_GA = _O_DV + D_MODEL
_O_GB = _O_GA + D_MODEL
_W_MAIN_COLS = _O_GB + D_MODEL


def _dot(a, b):
    return jnp.dot(a, b, preferred_element_type=F32)


def _sigmoid(x):
    return 1.0 / (1.0 + jnp.exp(-x))


def _row_rms(x, g):
    ms = jnp.mean(x * x, axis=-1, keepdims=True)
    return (x * lax.rsqrt(ms + EPS)) * g


def _seg_rms(z, bd, g):
    ms = _dot((z * z).astype(BF16), bd)
    return (z * lax.rsqrt(ms + EPS)) * g


def _rope_block(y, tab_ref, half):
    return (y * tab_ref[0]
            + pltpu.roll(y, LANES - half, 1) * tab_ref[1]
            + pltpu.roll(y, half, 1) * tab_ref[2])


def _proj_kernel(x_ref, taba_ref, tabb_ref, gmix_ref, w_ref, gql_ref, wuq_ref, gkv_ref, gv_ref, bd_ref,
                 qcat_ref, ckv_ref, kr_ref, dq_ref, dk32_ref, dk16_ref, dv32_ref, dv16_ref, ga_ref, gb_ref):
    xn = _row_rms(x_ref[...], gmix_ref[...]).astype(BF16)
    bd_a = bd_ref[0]
    bd_b = bd_ref[1]

    ql = _row_rms(_dot(xn, w_ref[:, _O_Q:_O_Q + Q_LORA]), gql_ref[...]).astype(BF16)
    q = _dot(ql, wuq_ref[...])
    for h in range(N_HEADS):
        sl = slice(h * LANES, (h + 1) * LANES)
        qb = _rope_block(_seg_rms(q[:, sl], bd_a, gv_ref[0:1, :]), taba_ref, QK_ROPE // 2)
        qcat_ref[:, sl] = (qb * (MLA_SCALE * LOG2E)).astype(BF16)

    ckv_ref[...] = _row_rms(_dot(xn, w_ref[:, _O_CKV:_O_CKV + KV_LORA]), gkv_ref[...])
    kr = _dot(xn, w_ref[:, _O_KR:_O_KR + LANES])
    kr_ref[...] = _rope_block(_seg_rms(kr, bd_a, gv_ref[1:2, :]), taba_ref, QK_ROPE // 2)

    zq = _dot(xn, w_ref[:, _O_DQ:_O_DQ + D_MODEL])
    zk = _dot(xn, w_ref[:, _O_DK:_O_DK + D_MODEL])
    for h in range(N_HEADS):
        sl = slice(h * LANES, (h + 1) * LANES)
        qb = _rope_block(_seg_rms(zq[:, sl], bd_b, gv_ref[2:3, :]), tabb_ref, ROT_B // 2)
        dq_ref[:, sl] = (qb * (DIFF_SCALE * LOG2E)).astype(BF16)
        kb = _rope_block(_seg_rms(zk[:, sl], bd_b, gv_ref[3:4, :]), tabb_ref, ROT_B // 2)
        dk32_ref[:, sl] = kb
        dk16_ref[:, sl] = kb.astype(BF16)

    dv = _dot(xn, w_ref[:, _O_DV:_O_DV + D_MODEL])
    dv32_ref[...] = dv
    dv16_ref[...] = dv.astype(BF16)
    ga_ref[...] = _sigmoid(_dot(xn, w_ref[:, _O_GA:_O_GA + D_MODEL])).astype(BF16)
    gb_ref[...] = _sigmoid(_dot(xn, w_ref[:, _O_GB:_O_GB + D_MODEL])).astype(BF16)


def _const_spec(shape):
    nd = len(shape)
    return pl.BlockSpec(shape, lambda *_: (0,) * nd, pipeline_mode=pl.Buffered(1))


def _proj(x, tab_a, tab_b, wp):
    b, s, _ = x.shape
    tm = min(512, s)
    grid = (b, s // tm)

    def tok(width):
        return pl.BlockSpec((None, tm, width), lambda bi, i: (bi, i, 0))

    tab_spec = pl.BlockSpec((3, tm, LANES), lambda bi, i: (0, i, 0))
    out_shapes = [
        jax.ShapeDtypeStruct((b, s, D_MODEL), BF16),
        jax.ShapeDtypeStruct((b, s, KV_LORA), F32),
        jax.ShapeDtypeStruct((b, s, LANES), F32),
        jax.ShapeDtypeStruct((b, s, D_MODEL), BF16),
        jax.ShapeDtypeStruct((b, s, D_MODEL), F32),
        jax.ShapeDtypeStruct((b, s, D_MODEL), BF16),
        jax.ShapeDtypeStruct((b, s, D_MODEL), F32),
        jax.ShapeDtypeStruct((b, s, D_MODEL), BF16),
        jax.ShapeDtypeStruct((b, s, D_MODEL), BF16),
        jax.ShapeDtypeStruct((b, s, D_MODEL), BF16),
    ]
    out_specs = [tok(D_MODEL), tok(KV_LORA), tok(LANES), tok(D_MODEL), tok(D_MODEL), tok(D_MODEL),
                 tok(D_MODEL), tok(D_MODEL), tok(D_MODEL), tok(D_MODEL)]
    in_specs = [
        tok(D_MODEL), tab_spec, tab_spec,
        _const_spec((1, D_MODEL)), _const_spec((D_MODEL, _W_MAIN_COLS)),
        _const_spec((1, Q_LORA)), _const_spec((Q_LORA, D_MODEL)), _const_spec((1, KV_LORA)),
        _const_spec((8, LANES)), _const_spec((2, LANES, LANES)),
    ]
    return pl.pallas_call(
        _proj_kernel, grid=grid, in_specs=in_specs, out_specs=out_specs, out_shape=out_shapes,
        compiler_params=pltpu.CompilerParams(dimension_semantics=("parallel", "parallel"),
                                             vmem_limit_bytes=VMEM_LIMIT),
        name="proj",
    )(x, tab_a, tab_b, wp["g_mix"], wp["w_main"], wp["g_q_lat"], wp["w_uq"], wp["g_kv_lat"],
      wp["gvecs"], wp["bd"])


def _kvup_kernel(ckv_ref, kr_ref, wuk_ref, wuv_ref, gv_ref, bd_ref, kcat_ref, va_ref):
    c = ckv_ref[...].astype(BF16)
    kn = _dot(c, wuk_ref[...])
    kr = kr_ref[...]
    bd_a = bd_ref[0]
    for h in range(N_HEADS):
        sl = slice(h * LANES, (h + 1) * LANES)
        kcat_ref[:, sl] = (_seg_rms(kn[:, sl], bd_a, gv_ref[4:5, :]) + kr).astype(BF16)
    va_ref[...] = _dot(c, wuv_ref[...]).astype(BF16)


def _kv_up(ckv, kr, wp):
    m = ckv.shape[0]
    tm = 256
    assert m % tm == 0

    def tok(width):
        return pl.BlockSpec((tm, width), lambda i: (i, 0))

    return pl.pallas_call(
        _kvup_kernel, grid=(m // tm,),
        in_specs=[tok(KV_LORA), tok(LANES), _const_spec((KV_LORA, D_MODEL)), _const_spec((KV_LORA, D_MODEL)),
                  _const_spec((8, LANES)), _const_spec((2, LANES, LANES))],
        out_specs=[tok(D_MODEL), tok(D_MODEL)],
        out_shape=[jax.ShapeDtypeStruct((m, D_MODEL), BF16), jax.ShapeDtypeStruct((m, D_MODEL), BF16)],
        compiler_params=pltpu.CompilerParams(dimension_semantics=("parallel",), vmem_limit_bytes=VMEM_LIMIT),
        name="kv_up",
    )(ckv, kr, wp["w_uk"], wp["w_uv"], wp["gvecs"], wp["bd"])


def _attn_kernel(lamv_ref, gsub_ref, q_ref, k_ref, v_ref, o_ref, qt_sc, m_sc, l_sc, acc_sc, st0_sc, st1_sc,
                 *, maps, tq, q_off, lam_init, bounded):
    qi = pl.program_id(1)
    rows = maps * tq
    n_full = q_off // tq + qi

    eye = (lax.broadcasted_iota(jnp.int32, (LANES, LANES), 0)
           == lax.broadcasted_iota(jnp.int32, (LANES, LANES), 1)).astype(BF16)
    for h in range(N_HEADS):
        sl = slice(h * LANES, (h + 1) * LANES)
        qt = lax.dot_general(eye, q_ref[:, sl], (((1,), (1,)), ((), ())), preferred_element_type=F32).astype(BF16)
        if maps == 2:
            sub = lax.broadcasted_iota(jnp.int32, (LANES, tq), 0)
            zero = jnp.zeros_like(qt)
            qt = jnp.concatenate([jnp.where(sub < D_B, qt, zero), jnp.where(sub >= D_B, qt, zero)], axis=1)
        qt_sc[h] = qt
        m_sc[h] = jnp.full((1, rows), -jnp.inf, F32)
        l_sc[h] = jnp.zeros((1, rows), F32)
        acc_sc[h] = jnp.zeros((LANES, rows), F32)

    def scores(h, koff, buf):
        buf[h] = _dot(k_ref[pl.ds(koff, tq), h * LANES:(h + 1) * LANES], qt_sc[h])

    def softmax_pv(h, koff, buf, vis):
        sl = slice(h * LANES, (h + 1) * LANES)
        st = buf[h]
        if vis is not None:
            st = jnp.where(vis, st, -jnp.inf)
        if bounded:
            pt = jnp.exp2(st)
            l_sc[h] = l_sc[h] + jnp.sum(pt, axis=0, keepdims=True)
            acc_sc[h] = acc_sc[h] + lax.dot_general(v_ref[pl.ds(koff, tq), sl], pt.astype(BF16),
                                                    (((0,), (0,)), ((), ())), preferred_element_type=F32)
            return
        m_prev = m_sc[h]
        m_next = jnp.maximum(m_prev, jnp.max(st, axis=0, keepdims=True))
        alpha = jnp.exp2(m_prev - m_next)
        pt = jnp.exp2(st - m_next)
        l_sc[h] = alpha * l_sc[h] + jnp.sum(pt, axis=0, keepdims=True)
        pv = lax.dot_general(v_ref[pl.ds(koff, tq), sl], pt.astype(BF16), (((0,), (0,)), ((), ())),
                             preferred_element_type=F32)
        acc_sc[h] = alpha * acc_sc[h] + pv
        m_sc[h] = m_next

    for h in range(N_HEADS):
        scores(h, 0, st0_sc)

    def step(c, cur, nxt):
        koff = pl.multiple_of(c * tq, tq)
        knext = pl.multiple_of((c + 1) * tq, tq)
        scores(0, knext, nxt)
        for h in range(N_HEADS):
            if h + 1 < N_HEADS:
                scores(h + 1, knext, nxt)
            softmax_pv(h, koff, cur, None)

    def chunk_pair(i, carry):
        step(2 * i, st0_sc, st1_sc)
        step(2 * i + 1, st1_sc, st0_sc)
        return carry

    lax.fori_loop(0, n_full // 2, chunk_pair, 0)
    odd = n_full % 2 == 1

    @pl.when(odd)
    def _odd_tail():
        step(n_full - 1, st0_sc, st1_sc)

    kc = lax.broadcasted_iota(jnp.int32, (tq, rows), 0) >> 6
    qc = lax.broadcasted_iota(jnp.int32, (tq, rows), 1)
    if maps == 2:
        qc = jnp.where(qc >= tq, qc - tq, qc)
    vis = kc <= (qc >> 6)
    kdiag = pl.multiple_of(n_full * tq, tq)

    @pl.when(odd)
    def _diag_odd():
        for h in range(N_HEADS):
            softmax_pv(h, kdiag, st1_sc, vis)

    @pl.when(jnp.logical_not(odd))
    def _diag_even():
        for h in range(N_HEADS):
            softmax_pv(h, kdiag, st0_sc, vis)

    if maps == 2:
        lv = lamv_ref[...]
        lam = (jnp.exp(jnp.sum(lv[0:1] * lv[1:2], axis=-1, keepdims=True))
               - jnp.exp(jnp.sum(lv[2:3] * lv[3:4], axis=-1, keepdims=True)) + lam_init)
    eye_q = (lax.broadcasted_iota(jnp.int32, (tq, tq), 0)
             == lax.broadcasted_iota(jnp.int32, (tq, tq), 1)).astype(BF16)
    for h in range(N_HEADS):
        sl = slice(h * LANES, (h + 1) * LANES)
        ot = acc_sc[h] / l_sc[h]
        if maps == 2:
            ot = ot[:, :tq] - lam * ot[:, tq:]
            ms = jnp.mean(ot * ot, axis=0, keepdims=True)
            ot = ((ot * lax.rsqrt(ms + EPS)) * gsub_ref[...]) * (1.0 - lam_init)
        o = lax.dot_general(eye_q, ot.astype(BF16), (((1,), (1,)), ((), ())), preferred_element_type=F32)
        o_ref[:, sl] = o.astype(o_ref.dtype)


SCORE_BOUND_MAX = 40.0


def _attention(q, k, v, lamv, gsub, *, maps, q_off, lam_init, tq, bounded=False):
    b, sq, _ = q.shape
    sk = k.shape[1]
    assert sq % tq == 0 and q_off % tq == 0 and tq % CHUNK == 0 and sk == q_off + sq
    rows = maps * tq
    kern = functools.partial(_attn_kernel, maps=maps, tq=tq, q_off=q_off, lam_init=lam_init, bounded=bounded)
    kv_spec = pl.BlockSpec((None, sk, D_MODEL), lambda bi, qi: (bi, 0, 0), pipeline_mode=pl.Buffered(1))
    return pl.pallas_call(
        kern, grid=(b, sq // tq),
        in_specs=[_const_spec((4, LANES)), _const_spec((LANES, 1)),
                  pl.BlockSpec((None, tq, D_MODEL), lambda bi, qi: (bi, qi, 0)), kv_spec, kv_spec],
        out_specs=pl.BlockSpec((None, tq, D_MODEL), lambda bi, qi: (bi, qi, 0)),
        out_shape=jax.ShapeDtypeStruct((b, sq, D_MODEL), BF16),
        scratch_shapes=[pltpu.VMEM((N_HEADS, LANES, rows), BF16), pltpu.VMEM((N_HEADS, 1, rows), F32),
                        pltpu.VMEM((N_HEADS, 1, rows), F32), pltpu.VMEM((N_HEADS, LANES, rows), F32),
                        pltpu.VMEM((N_HEADS, tq, rows), F32), pltpu.VMEM((N_HEADS, tq, rows), F32)],
        compiler_params=pltpu.CompilerParams(dimension_semantics=("parallel", "arbitrary"),
                                             vmem_limit_bytes=VMEM_LIMIT),
        name=("attn_diff" if maps == 2 else "attn_mla") + ("_bounded" if bounded else ""),
    )(lamv, gsub, q, k, v)


def _merge_kernel(x_ref, oa_ref, ob_ref, ga_ref, gb_ref, wout_ref, gffn_ref, wrh_ref, wrl_ref, br_ref,
                  h_ref, hn_ref, route_ref, counts_ref, cnt_sc):
    merged = (ga_ref[...].astype(F32) * oa_ref[...].astype(F32)
              + gb_ref[...].astype(F32) * ob_ref[...].astype(F32))
    h = x_ref[...] + _dot(merged.astype(BF16), wout_ref[...])
    h_ref[...] = h
    hn = _row_rms(h, gffn_ref[...])
    hn_ref[...] = hn

    hi = hn.astype(BF16)
    lo = (hn - hi.astype(F32)).astype(BF16)
    logits = _dot(hi, wrh_ref[...]) + (_dot(lo, wrh_ref[...]) + _dot(hi, wrl_ref[...])) + br_ref[...]
    lane = lax.broadcasted_iota(jnp.int32, logits.shape, 1).astype(F32)
    neg = jnp.full_like(logits, -jnp.inf)
    big = jnp.full_like(logits, 1e9)

    def first_argmax(vals):
        vmax = jnp.max(vals, axis=1, keepdims=True)
        return vmax, jnp.min(jnp.where(vals == vmax, lane, big), axis=1, keepdims=True)

    gmask = lane < N_GROUPS
    gmax, gsel = first_argmax(jnp.where(gmask, logits, neg))
    p_top = 1.0 / jnp.sum(jnp.where(gmask, jnp.exp(logits - gmax), 0.0), axis=1, keepdims=True)
    e_lo = N_GROUPS + gsel * EXP_PER_GROUP
    emask = jnp.logical_and(lane >= e_lo, lane < e_lo + EXP_PER_GROUP)
    le = jnp.where(emask, logits, neg)
    v1, i1 = first_argmax(le)
    v2, i2 = first_argmax(jnp.where(lane == i1, neg, le))
    t = jnp.exp(v2 - v1)
    w1 = p_top / (1.0 + t)
    w2 = p_top * t / (1.0 + t)
    e1 = i1 - N_GROUPS
    e2 = i2 - N_GROUPS

    @pl.when(pl.program_id(0) == 0)
    def _zero_counts():
        cnt_sc[...] = jnp.zeros(cnt_sc.shape, F32)

    tm = logits.shape[0]
    oh1 = lane == e1
    oh2 = lane == e2
    earlier = (lax.broadcasted_iota(jnp.int32, (tm, tm), 1)
               < lax.broadcasted_iota(jnp.int32, (tm, tm), 0)).astype(BF16)
    before1 = _dot(earlier, oh1.astype(BF16))
    before2 = _dot(earlier, oh2.astype(BF16))
    base = cnt_sc[...]
    c1 = jnp.sum(oh1.astype(F32), axis=0, keepdims=True)
    c2 = jnp.sum(oh2.astype(F32), axis=0, keepdims=True)
    rank1 = jnp.sum(jnp.where(oh1, base + before1, 0.0), axis=1, keepdims=True)
    rank2 = jnp.sum(jnp.where(oh2, (base + c1) + before2, 0.0), axis=1, keepdims=True)
    total = base + (c1 + c2)
    cnt_sc[...] = total
    counts_ref[...] = total

    route = jnp.where(lane == 0, e1, 0.0)
    route = jnp.where(lane == 1, e2, route)
    route = jnp.where(lane == 2, w1, route)
    route = jnp.where(lane == 3, w2, route)
    route = jnp.where(lane == 4, rank1, route)
    route = jnp.where(lane == 5, rank2, route)
    route_ref[...] = route


def _merge(x, oa, ob, ga, gb, wp):
    n = x.shape[0]
    tm = 256
    assert n % tm == 0

    def tok(width):
        return pl.BlockSpec((tm, width), lambda i: (i, 0))

    return pl.pallas_call(
        _merge_kernel, grid=(n // tm,),
        in_specs=[tok(D_MODEL)] * 5 + [_const_spec((D_MODEL, D_MODEL)), _const_spec((1, D_MODEL)),
                                       _const_spec((D_MODEL, LANES)), _const_spec((D_MODEL, LANES)),
                                       _const_spec((1, LANES))],
        out_specs=[tok(D_MODEL), tok(D_MODEL), tok(LANES), pl.BlockSpec((1, LANES), lambda i: (0, 0))],
        out_shape=[jax.ShapeDtypeStruct((n, D_MODEL), F32), jax.ShapeDtypeStruct((n, D_MODEL), F32),
                   jax.ShapeDtypeStruct((n, LANES), F32), jax.ShapeDtypeStruct((1, LANES), F32)],
        scratch_shapes=[pltpu.VMEM((1, LANES), F32)],
        compiler_params=pltpu.CompilerParams(dimension_semantics=("arbitrary",), vmem_limit_bytes=VMEM_LIMIT),
        name="merge",
    )(x, oa, ob, ga, gb, wp["w_out"], wp["g_ffn"], wp["w_r_hi"], wp["w_r_lo"], wp["b_r"])


MOE_TILE = 256


def _gather_rows(idx_ref, n_rows, src_hbm, dst_ref, sem, *, unrolled):
    def start(r):
        pltpu.make_async_copy(src_hbm.at[pl.ds(idx_ref[0, 0, r], 1), :], dst_ref.at[pl.ds(r, 1), :], sem).start()

    if unrolled:
        for r in range(n_rows):
            start(r)
    else:
        def body(r, carry):
            start(r)
            return carry
        lax.fori_loop(0, n_rows, body, 0, unroll=8)


def _wait_rows(n_rows, src_hbm, dst_ref, sem):
    pltpu.make_async_copy(src_hbm.at[pl.ds(0, n_rows), :], dst_ref, sem).wait()


DISPATCH_TILE = 256


DISPATCH_SLOTS = 3


def _dispatch_kernel(pos_ref, hn_hbm, xs_hbm, buf, load_sem, scat_sem):
    i = pl.program_id(0)
    n = pl.num_programs(0)
    td = DISPATCH_TILE
    slot = i % DISPATCH_SLOTS

    def load(t, s):
        return pltpu.make_async_copy(hn_hbm.at[pl.ds(pl.multiple_of(t * td, td), td), :], buf.at[s], load_sem.at[s])

    def drain(s):
        pltpu.make_async_copy(xs_hbm.at[pl.ds(0, 2 * td), :], xs_hbm.at[pl.ds(0, 2 * td), :], scat_sem.at[s]).wait()

    @pl.when(i == 0)
    def _first_load():
        load(0, 0).start()

    @pl.when(i + 1 < n)
    def _next_load():
        nxt = (i + 1) % DISPATCH_SLOTS

        @pl.when(i >= 2)
        def _free_slot():
            drain(nxt)

        load(i + 1, nxt).start()

    load(i, slot).wait()
    for r in range(td):
        for c in range(2):
            dst = xs_hbm.at[pl.ds(pos_ref[0, 0, c * td + r], 1), :]
            pltpu.make_async_copy(buf.at[slot, pl.ds(r, 1), :], dst, scat_sem.at[slot]).start()

    @pl.when(i == n - 1)
    def _drain_all():
        @pl.when(i >= 2)
        def _():
            drain((i + 1) % DISPATCH_SLOTS)

        @pl.when(i >= 1)
        def _():
            drain((i + 2) % DISPATCH_SLOTS)

        drain(slot)


def _dispatch(hn, pos3):
    n = hn.shape[0]
    assert n % DISPATCH_TILE == 0
    return pl.pallas_call(
        _dispatch_kernel, grid=(n // DISPATCH_TILE,),
        in_specs=[pl.BlockSpec((1, 1, 2 * DISPATCH_TILE), lambda i: (i, 0, 0), memory_space=pltpu.SMEM),
                  pl.BlockSpec(memory_space=pl.ANY)],
        out_specs=pl.BlockSpec(memory_space=pl.ANY),
        out_shape=jax.ShapeDtypeStruct((2 * n, D_MODEL), F32),
        scratch_shapes=[pltpu.VMEM((DISPATCH_SLOTS, DISPATCH_TILE, D_MODEL), F32),
                        pltpu.SemaphoreType.DMA((DISPATCH_SLOTS,)), pltpu.SemaphoreType.DMA((DISPATCH_SLOTS,))],
        compiler_params=pltpu.CompilerParams(dimension_semantics=("arbitrary",)),
        name="dispatch",
    )(pos3, hn)


def _moe_kernel(vtile_ref, vexp_ref, vlo_ref, vhi_ref, vfirst_ref, xs_ref, wei_ref, weo_ref, y_ref):
    del vexp_ref
    v = pl.program_id(0)
    x = xs_ref[...].astype(BF16)
    ab = _dot(x, wei_ref[...])
    a = ab[:, :D_EXPERT]
    act = (a * _sigmoid(a)) * ab[:, D_EXPERT:]
    y = _dot(act.astype(BF16), weo_ref[...])

    @pl.when(vfirst_ref[v] == 1)
    def _first_visit():
        y_ref[...] = y

    @pl.when(vfirst_ref[v] == 0)
    def _later_visit():
        row = vtile_ref[v] * MOE_TILE + lax.broadcasted_iota(jnp.int32, (MOE_TILE, 1), 0)
        mine = jnp.logical_and(row >= vlo_ref[v], row < vhi_ref[v])
        y_ref[...] = jnp.where(mine, y, y_ref[...])


def _moe(xs, visits, wp):
    n_visits = visits[0].shape[0]
    grid_spec = pltpu.PrefetchScalarGridSpec(
        num_scalar_prefetch=5, grid=(n_visits,),
        in_specs=[pl.BlockSpec((MOE_TILE, D_MODEL), lambda v, vt, ve, lo, hi, fi: (vt[v], 0)),
                  pl.BlockSpec((None, D_MODEL, 2 * D_EXPERT), lambda v, vt, ve, lo, hi, fi: (ve[v], 0, 0)),
                  pl.BlockSpec((None, D_EXPERT, D_MODEL), lambda v, vt, ve, lo, hi, fi: (ve[v], 0, 0))],
        out_specs=pl.BlockSpec((MOE_TILE, D_MODEL), lambda v, vt, ve, lo, hi, fi: (vt[v], 0)),
    )
    return pl.pallas_call(
        _moe_kernel, grid_spec=grid_spec,
        out_shape=jax.ShapeDtypeStruct(xs.shape, F32),
        compiler_params=pltpu.CompilerParams(dimension_semantics=("arbitrary",), vmem_limit_bytes=VMEM_LIMIT),
        name="moe",
    )(*visits, xs, wp["w_exp_in"], wp["w_exp_out"])


def _routing_tables(route, counts, n_visits):
    e = route[:, 0:2].astype(jnp.int32)
    rank = route[:, 4:6].astype(jnp.int32)
    cnt = counts[0, :N_EXPERTS].astype(jnp.int32)
    end = jnp.cumsum(cnt)
    start = end - cnt
    ids = jnp.arange(N_EXPERTS, dtype=jnp.int32)
    pos = rank + jnp.sum(jnp.where(e[:, :, None] == ids[None, None, :], start[None, None, :], 0), axis=2)

    first_tile = start // MOE_TILE
    n_vis = jnp.where(cnt > 0, (end - 1) // MOE_TILE - first_tile + 1, 0)
    vis_end = jnp.cumsum(n_vis)
    vis_start = vis_end - n_vis
    v = jnp.minimum(jnp.arange(n_visits, dtype=jnp.int32), vis_end[-1] - 1)
    v_exp = jnp.sum((v[:, None] >= vis_end[None, :]).astype(jnp.int32), axis=1)
    pick = lambda table: jnp.sum(jnp.where(v_exp[:, None] == ids[None, :], table[None, :], 0), axis=1)
    v_tile = pick(first_tile) + (v - pick(vis_start))
    v_first = jnp.concatenate([jnp.ones((1,), jnp.int32), (v_tile[1:] != v_tile[:-1]).astype(jnp.int32)])
    visits = tuple(a.astype(jnp.int32) for a in (v_tile, v_exp, pick(start), pick(end), v_first))
    return pos, visits


COMB_TILE = 256


def _combine_kernel(pos_ref, posn_ref, route_ref, h_ref, pe_ref, y_hbm, gple_ref, wg_ref, wp_ref,
                    o_ref, ybuf, sem):
    i = pl.program_id(0)
    n = pl.num_programs(0)
    slot = i % 2
    rows = 2 * COMB_TILE

    @pl.when(i == 0)
    def _first():
        _gather_rows(pos_ref, rows, y_hbm, ybuf.at[0], sem.at[0], unrolled=False)

    @pl.when(i + 1 < n)
    def _prefetch():
        _gather_rows(posn_ref, rows, y_hbm, ybuf.at[1 - slot], sem.at[1 - slot], unrolled=True)

    _wait_rows(rows, y_hbm, ybuf.at[slot], sem.at[slot])
    route = route_ref[...]
    h = h_ref[...] + (route[:, 2:3] * ybuf[slot, 0:COMB_TILE, :] + route[:, 3:4] * ybuf[slot, COMB_TILE:rows, :])
    gate = _sigmoid(_dot(_row_rms(h, gple_ref[...]).astype(BF16), wg_ref[...]))
    o_ref[...] = h + gate * _dot(pe_ref[...].astype(BF16), wp_ref[...])


def _combine(route, h, pe, y, pos3, wp):
    n = h.shape[0]
    assert n % COMB_TILE == 0 and COMB_TILE == DISPATCH_TILE
    n_tiles = n // COMB_TILE

    def tok(width):
        return pl.BlockSpec((COMB_TILE, width), lambda i: (i, 0))

    pos_spec = pl.BlockSpec((1, 1, 2 * COMB_TILE), lambda i: (i, 0, 0), memory_space=pltpu.SMEM)
    posn_spec = pl.BlockSpec((1, 1, 2 * COMB_TILE), lambda i: (jnp.minimum(i + 1, n_tiles - 1), 0, 0),
                             memory_space=pltpu.SMEM)
    return pl.pallas_call(
        _combine_kernel, grid=(n_tiles,),
        in_specs=[pos_spec, posn_spec, tok(LANES), tok(D_MODEL), tok(D_PLE), pl.BlockSpec(memory_space=pl.ANY),
                  _const_spec((1, D_MODEL)), _const_spec((D_MODEL, D_MODEL)), _const_spec((D_PLE, D_MODEL))],
        out_specs=tok(D_MODEL),
        out_shape=jax.ShapeDtypeStruct((n, D_MODEL), F32),
        scratch_shapes=[pltpu.VMEM((2, 2 * COMB_TILE, D_MODEL), F32), pltpu.SemaphoreType.DMA((2,))],
        compiler_params=pltpu.CompilerParams(dimension_semantics=("arbitrary",), vmem_limit_bytes=VMEM_LIMIT),
        name="combine",
    )(pos3, pos3, route, h, pe, y, wp["g_ple"], wp["w_ple_gate"], wp["w_ple_proj"])


def _rope_tables(pos, rot, offsets):
    half = rot // 2
    inv = ROPE_THETA ** (-jnp.arange(half, dtype=F32) * (2.0 / rot))
    lane = jnp.arange(LANES)
    rel = jnp.full((LANES,), -1)
    for o in offsets:
        rel = jnp.where((lane >= o) & (lane < o + rot), lane - o, rel)
    first, second = (rel >= 0) & (rel < half), rel >= half
    ang = pos.astype(F32)[:, None] * inv[jnp.maximum(rel, 0) % half][None, :]
    c, s = jnp.cos(ang), jnp.sin(ang)
    return jnp.stack([jnp.where(rel >= 0, c, 1.0), jnp.where(first, -s, 0.0), jnp.where(second, s, 0.0)])


def _prep_weights(l, g_mix, w_in, g_q_lat, w_uq, g_kv_lat, w_uk, w_uv, g_mla_qn, g_mla_qr, g_mla_kn, g_mla_kr,
                  g_diff_q, g_diff_k, lambda_q1, lambda_k1, lambda_q2, lambda_k2, g_diff_sub, w_out,
                  g_ffn, w_router_grp, b_router_grp, w_router_exp, b_router_exp, w_exp_in, w_exp_out,
                  g_ple, w_ple_gate, w_ple_proj):
    offs = [0]
    for sz in IN_SIZES:
        offs.append(offs[-1] + sz)
    wi = w_in[l]
    seg = [wi[:, offs[j]:offs[j + 1]] for j in range(len(IN_SIZES))]
    w_kr_pad = jnp.pad(seg[2], ((0, 0), (0, LANES - QK_ROPE)))
    w_main = jnp.concatenate([seg[0], seg[1], w_kr_pad] + seg[3:], axis=1).astype(BF16)

    hd = QK_NOPE + QK_ROPE
    uq = w_uq[l].reshape(Q_LORA, H_A, hd)
    uq_pad = jnp.concatenate([uq[:, :, QK_NOPE:], jnp.zeros((Q_LORA, H_A, LANES - hd), F32), uq[:, :, :QK_NOPE]],
                             axis=2).reshape(Q_LORA, H_A * LANES).astype(BF16)
    uk = w_uk[l].reshape(KV_LORA, H_A, QK_NOPE)
    uk_pad = jnp.concatenate([jnp.zeros((KV_LORA, H_A, LANES - QK_NOPE), F32), uk], axis=2)
    uk_pad = uk_pad.reshape(KV_LORA, H_A * LANES).astype(BF16)

    z32 = jnp.zeros((LANES - hd,), F32)
    gvecs = jnp.stack([
        jnp.concatenate([g_mla_qr[l], z32, g_mla_qn[l]]),
        jnp.concatenate([g_mla_kr[l], jnp.zeros((LANES - QK_ROPE,), F32)]),
        jnp.concatenate([g_diff_q[l], g_diff_q[l]]),
        jnp.concatenate([g_diff_k[l], g_diff_k[l]]),
        jnp.concatenate([jnp.zeros((LANES - QK_NOPE,), F32), g_mla_kn[l]]),
        jnp.zeros((LANES,), F32), jnp.zeros((LANES,), F32), jnp.zeros((LANES,), F32)])

    lane = jnp.arange(LANES)
    seg_a = jnp.where(lane < QK_ROPE, 0, jnp.where(lane < LANES - QK_NOPE, -1, 1))
    same_a = (seg_a[:, None] == seg_a[None, :]) & (seg_a[:, None] >= 0)
    bd_a = jnp.where(same_a, jnp.where(seg_a[:, None] == 0, 1.0 / QK_ROPE, 1.0 / QK_NOPE), 0.0)
    seg_b = lane // D_B
    bd_b = jnp.where(seg_b[:, None] == seg_b[None, :], 1.0 / D_B, 0.0)
    bd = jnp.stack([bd_a, bd_b]).astype(BF16)

    w_r = jnp.concatenate([w_router_grp[l], w_router_exp[l],
                           jnp.zeros((D_MODEL, LANES - N_GROUPS - N_EXPERTS), F32)], axis=1)
    w_r_hi = w_r.astype(BF16)
    w_r_lo = (w_r - w_r_hi.astype(F32)).astype(BF16)
    b_r = jnp.concatenate([b_router_grp[l], b_router_exp[l],
                           jnp.zeros((LANES - N_GROUPS - N_EXPERTS,), F32)]).reshape(1, LANES)

    amax = lambda g: jnp.max(jnp.abs(g[l]))
    bound_diff = 1.05 * DIFF_SCALE * LOG2E * D_B * amax(g_diff_q) * amax(g_diff_k)
    bound_mla = 1.05 * MLA_SCALE * LOG2E * (
        jnp.sqrt(QK_NOPE * amax(g_mla_qn) ** 2 + QK_ROPE * amax(g_mla_qr) ** 2)
        * jnp.sqrt(QK_NOPE * amax(g_mla_kn) ** 2 + QK_ROPE * amax(g_mla_kr) ** 2))

    pad64 = jnp.zeros((LANES - D_B,), F32)
    lamv = jnp.stack([jnp.concatenate([v[l], pad64]) for v in (lambda_q1, lambda_k1, lambda_q2, lambda_k2)])

    return {
        "g_mix": g_mix[l].reshape(1, D_MODEL), "w_main": w_main,
        "g_q_lat": g_q_lat[l].reshape(1, Q_LORA), "w_uq": uq_pad,
        "g_kv_lat": g_kv_lat[l].reshape(1, KV_LORA), "w_uk": uk_pad, "w_uv": w_uv[l].astype(BF16),
        "gvecs": gvecs, "bd": bd, "lamv": lamv, "bound_mla": bound_mla, "bound_diff": bound_diff, "g_diff_sub": g_diff_sub[l].reshape(V_B, 1),
        "w_out": w_out[l].astype(BF16), "g_ffn": g_ffn[l].reshape(1, D_MODEL),
        "w_r_hi": w_r_hi, "w_r_lo": w_r_lo, "b_r": b_r,
        "w_exp_in": w_exp_in[l].astype(BF16), "w_exp_out": w_exp_out[l].astype(BF16),
        "g_ple": g_ple[l].reshape(1, D_MODEL), "w_ple_gate": w_ple_gate[l].astype(BF16),
        "w_ple_proj": w_ple_proj[l].astype(BF16),
    }


def _layer(x, pe, pos, past, wp, lam_init):
    b, s, _ = x.shape
    tab_a = _rope_tables(pos, QK_ROPE, (0,))
    tab_b = _rope_tables(pos, ROT_B, (0, D_B))
    q_cat, ckv, kr, dq, dk32, dk16, dv32, dv16, ga, gb = _proj(x, tab_a, tab_b, wp)

    if past is None:
        ckv_all, kr_all, dk_all, dv_all = ckv, kr, dk16, dv16
        q_off = 0
        tq = min(256, s)
    else:
        ckv_p, kr_p, dk_p, dv_p = past
        past_len = ckv_p.shape[1]
        ckv_all = jnp.concatenate([ckv_p, ckv], axis=1)
        kr_all = jnp.concatenate([jnp.pad(kr_p, ((0, 0), (0, 0), (0, LANES - QK_ROPE))), kr], axis=1)
        dk_all = jnp.concatenate([dk_p.reshape(b, past_len, D_MODEL).astype(BF16), dk16], axis=1)
        dv_all = jnp.concatenate([dv_p.reshape(b, past_len, D_MODEL).astype(BF16), dv16], axis=1)
        q_off = past_len
        tq = s
    sk = ckv_all.shape[1]

    k_cat, v_a = _kv_up(ckv_all.reshape(b * sk, KV_LORA), kr_all.reshape(b * sk, LANES), wp)

    def attend(q, k, v, maps, score_bound):
        run = functools.partial(_attention, maps=maps, q_off=q_off, lam_init=lam_init, tq=tq)
        args = (q, k, v, wp["lamv"], wp["g_diff_sub"])
        if past is not None:
            return run(*args)
        return lax.cond(score_bound <= SCORE_BOUND_MAX,
                        lambda *a: run(*a, bounded=True), lambda *a: run(*a, bounded=False), *args)

    o_a = attend(q_cat, k_cat.reshape(b, sk, D_MODEL), v_a.reshape(b, sk, D_MODEL), 1, wp["bound_mla"])
    o_b = attend(dq, dk_all, dv_all, 2, wp["bound_diff"])

    n = b * s
    flat = lambda a: a.reshape(n, a.shape[-1])
    h, hn, route, counts = _merge(flat(x), flat(o_a), flat(o_b), flat(ga), flat(gb), wp)

    slot_of, visits = _routing_tables(route, counts, (2 * n) // MOE_TILE + N_EXPERTS - 1)
    slots3 = slot_of.reshape(n // COMB_TILE, COMB_TILE, 2).transpose(0, 2, 1).reshape(n // COMB_TILE, 1, 2 * COMB_TILE)
    y = _moe(_dispatch(hn, slots3), visits, wp)
    out = _combine(route, h, flat(pe), y, slots3, wp)

    return (out.reshape(b, s, D_MODEL),
            (ckv, kr[:, :, :QK_ROPE], dk32.reshape(b, s, H_B, 2, D_B), dv32.reshape(b, s, H_B, V_B)))


def kernel(x_prompt, x_sample, p_prompt, p_sample, cache_mla_ckv, cache_mla_krope, cache_diff_k, cache_diff_v,
           g_mix, w_in, g_q_lat, w_uq, g_kv_lat, w_uk, w_uv, g_mla_qn, g_mla_qr, g_mla_kn, g_mla_kr,
           g_diff_q, g_diff_k, lambda_q1, lambda_k1, lambda_q2, lambda_k2, g_diff_sub, w_out,
           g_ffn, w_router_grp, b_router_grp, w_router_exp, b_router_exp, w_exp_in, w_exp_out,
           g_ple, w_ple_gate, w_ple_proj):
    depth = w_in.shape[0]
    pos_p = jnp.arange(x_prompt.shape[1], dtype=jnp.int32)
    pos_s = cache_mla_ckv.shape[2] + jnp.arange(x_sample.shape[1], dtype=jnp.int32)
    hp, hs = x_prompt, x_sample
    st_p, st_s = [], []
    for l in range(depth):
        wp = _prep_weights(l, g_mix, w_in, g_q_lat, w_uq, g_kv_lat, w_uk, w_uv, g_mla_qn, g_mla_qr, g_mla_kn,
                           g_mla_kr, g_diff_q, g_diff_k, lambda_q1, lambda_k1, lambda_q2, lambda_k2, g_diff_sub,
                           w_out, g_ffn, w_router_grp, b_router_grp, w_router_exp, b_router_exp, w_exp_in,
                           w_exp_out, g_ple, w_ple_gate, w_ple_proj)
        lam_init = 0.8 - 0.6 * math.exp(-0.3 * l)
        hp, sp = _layer(hp, p_prompt[l], pos_p, None, wp, lam_init)
        hs, ss = _layer(hs, p_sample[l], pos_s,
                        (cache_mla_ckv[l], cache_mla_krope[l], cache_diff_k[l], cache_diff_v[l]), wp, lam_init)
        st_p.append(sp)
        st_s.append(ss)
    stack = lambda sts, j: jnp.stack([st[j] for st in sts])
    return (hp, hs,
            stack(st_p, 0), stack(st_p, 1), stack(st_p, 2), stack(st_p, 3),
            stack(st_s, 0), stack(st_s, 1), stack(st_s, 2), stack(st_s, 3))
```

```python
import functools
import math

import jax
import jax.numpy as jnp
from jax import lax
from jax.experimental import pallas as pl
from jax.experimental.pallas import tpu as pltpu

F32 = jnp.float32
BF16 = jnp.bfloat16

D_MODEL = 1024
CHUNK = 64
ROPE_THETA = 500000.0
EPS = 1e-6
H_A = 8
Q_LORA = 384
KV_LORA = 256
QK_NOPE = 64
QK_ROPE = 32
V_A = 128
MLA_SCALE = 1.0 / math.sqrt(QK_NOPE + QK_ROPE)
H_B = 8
D_B = 64
V_B = 2 * D_B
ROT_B = D_B // 4
DIFF_SCALE = 1.0 / math.sqrt(D_B)
LOG2E = math.log2(math.e)
N_GROUPS = 4
EXP_PER_GROUP = 8
N_EXPERTS = N_GROUPS * EXP_PER_GROUP
D_EXPERT = 256
D_PLE = 256
IN_SIZES = (Q_LORA, KV_LORA, QK_ROPE, H_B * 2 * D_B, H_B * 2 * D_B, H_B * V_B, D_MODEL, D_MODEL)

LANES = 128
N_HEADS = 8
VMEM_LIMIT = 56 * 1024 * 1024

_O_Q = 0
_O_CKV = _O_Q + Q_LORA
_O_KR = _O_CKV + KV_LORA
_O_DQ = _O_KR + LANES
_O_DK = _O_DQ + D_MODEL
_O_DV = _O_DK + D_MODEL
_O_GA = _O_DV + D_MODEL
_O_GB = _O_GA + D_MODEL
_W_MAIN_COLS = _O_GB + D_MODEL


def _dot(a, b):
    return jnp.dot(a, b, preferred_element_type=F32)


def _sigmoid(x):
    return 1.0 / (1.0 + jnp.exp(-x))


def _row_rms(x, g):
    ms = jnp.mean(x * x, axis=-1, keepdims=True)
    return (x * lax.rsqrt(ms + EPS)) * g


def _seg_rms(z, bd, g):
    ms = _dot((z * z).astype(BF16), bd)
    return (z * lax.rsqrt(ms + EPS)) * g


def _seg_rms_heads(z, bd2, g):
    g2 = jnp.concatenate([g, g], axis=1)
    blocks = []
    for p in range(z.shape[1] // (2 * LANES)):
        y = _seg_rms(z[:, p * 2 * LANES:(p + 1) * 2 * LANES], bd2, g2)
        blocks += [y[:, :LANES], y[:, LANES:]]
    return blocks


def _rope_block(y, tab_ref, half):
    return (y * tab_ref[0]
            + pltpu.roll(y, LANES - half, 1) * tab_ref[1]
            + pltpu.roll(y, half, 1) * tab_ref[2])


def _proj_kernel(x_ref, taba_ref, tabb_ref, gmix_ref, w_ref, gql_ref, wuq_ref, gkv_ref, gv_ref, bd_ref,
                 qcat_ref, ckv_ref, kr_ref, dq_ref, dk32_ref, dk16_ref, dv32_ref, dv16_ref, ga_ref, gb_ref):
    xn = _row_rms(x_ref[...], gmix_ref[...]).astype(BF16)
    bd_a = bd_ref[0]
    bd_b = bd_ref[1]

    ql = _row_rms(_dot(xn, w_ref[:, _O_Q:_O_Q + Q_LORA]), gql_ref[...]).astype(BF16)
    q_heads = _seg_rms_heads(_dot(ql, wuq_ref[...]), bd_a, gv_ref[0:1, :])
    for h in range(N_HEADS):
        sl = slice(h * LANES, (h + 1) * LANES)
        qb = _rope_block(q_heads[h], taba_ref, QK_ROPE // 2)
        qcat_ref[:, sl] = (qb * (MLA_SCALE * LOG2E)).astype(BF16)

    ckv_ref[...] = _row_rms(_dot(xn, w_ref[:, _O_CKV:_O_CKV + KV_LORA]), gkv_ref[...])
    kr = _dot(xn, w_ref[:, _O_KR:_O_KR + LANES])
    kr_ref[...] = _rope_block(_seg_rms(kr, bd_ref[0, :LANES, :LANES], gv_ref[1:2, :]), taba_ref, QK_ROPE // 2)

    zq_heads = _seg_rms_heads(_dot(xn, w_ref[:, _O_DQ:_O_DQ + D_MODEL]), bd_b, gv_ref[2:3, :])
    zk_heads = _seg_rms_heads(_dot(xn, w_ref[:, _O_DK:_O_DK + D_MODEL]), bd_b, gv_ref[3:4, :])
    for h in range(N_HEADS):
        sl = slice(h * LANES, (h + 1) * LANES)
        qb = _rope_block(zq_heads[h], tabb_ref, ROT_B // 2)
        dq_ref[:, sl] = (qb * (DIFF_SCALE * LOG2E)).astype(BF16)
        kb = _rope_block(zk_heads[h], tabb_ref, ROT_B // 2)
        dk32_ref[:, sl] = kb
        dk16_ref[:, sl] = kb.astype(BF16)

    dv = _dot(xn, w_ref[:, _O_DV:_O_DV + D_MODEL])
    dv32_ref[...] = dv
    dv16_ref[...] = dv.astype(BF16)
    ga_ref[...] = _sigmoid(_dot(xn, w_ref[:, _O_GA:_O_GA + D_MODEL])).astype(BF16)
    gb_ref[...] = _sigmoid(_dot(xn, w_ref[:, _O_GB:_O_GB + D_MODEL])).astype(BF16)


def _const_spec(shape):
    nd = len(shape)
    return pl.BlockSpec(shape, lambda *_: (0,) * nd, pipeline_mode=pl.Buffered(1))


def _proj(x, tab_a, tab_b, wp):
    b, s, _ = x.shape
    tm = min(512, s)
    grid = (b, s // tm)

    def tok(width):
        return pl.BlockSpec((None, tm, width), lambda bi, i: (bi, i, 0))

    tab_spec = pl.BlockSpec((3, tm, LANES), lambda bi, i: (0, i, 0))
    out_shapes = [
        jax.ShapeDtypeStruct((b, s, D_MODEL), BF16),
        jax.ShapeDtypeStruct((b, s, KV_LORA), F32),
        jax.ShapeDtypeStruct((b, s, LANES), F32),
        jax.ShapeDtypeStruct((b, s, D_MODEL), BF16),
        jax.ShapeDtypeStruct((b, s, D_MODEL), F32),
        jax.ShapeDtypeStruct((b, s, D_MODEL), BF16),
        jax.ShapeDtypeStruct((b, s, D_MODEL), F32),
        jax.ShapeDtypeStruct((b, s, D_MODEL), BF16),
        jax.ShapeDtypeStruct((b, s, D_MODEL), BF16),
        jax.ShapeDtypeStruct((b, s, D_MODEL), BF16),
    ]
    out_specs = [tok(D_MODEL), tok(KV_LORA), tok(LANES), tok(D_MODEL), tok(D_MODEL), tok(D_MODEL),
                 tok(D_MODEL), tok(D_MODEL), tok(D_MODEL), tok(D_MODEL)]
    in_specs = [
        tok(D_MODEL), tab_spec, tab_spec,
        _const_spec((1, D_MODEL)), _const_spec((D_MODEL, _W_MAIN_COLS)),
        _const_spec((1, Q_LORA)), _const_spec((Q_LORA, D_MODEL)), _const_spec((1, KV_LORA)),
        _const_spec((8, LANES)), _const_spec((2, 2 * LANES, 2 * LANES)),
    ]
    return pl.pallas_call(
        _proj_kernel, grid=grid, in_specs=in_specs, out_specs=out_specs, out_shape=out_shapes,
        compiler_params=pltpu.CompilerParams(dimension_semantics=("parallel", "parallel"),
                                             vmem_limit_bytes=VMEM_LIMIT),
        name="proj",
    )(x, tab_a, tab_b, wp["g_mix"], wp["w_main"], wp["g_q_lat"], wp["w_uq"], wp["g_kv_lat"],
      wp["gvecs"], wp["bd"])


def _kvup_kernel(ckv_ref, kr_ref, wuk_ref, wuv_ref, gv_ref, bd_ref, kcat_ref, va_ref):
    c = ckv_ref[...].astype(BF16)
    kn = _dot(c, wuk_ref[...])
    kr = kr_ref[...]
    kn_heads = _seg_rms_heads(kn, bd_ref[0], gv_ref[4:5, :])
    for h in range(N_HEADS):
        sl = slice(h * LANES, (h + 1) * LANES)
        kcat_ref[:, sl] = (kn_heads[h] + kr).astype(BF16)
    va_ref[...] = _dot(c, wuv_ref[...]).astype(BF16)


def _kv_up(ckv, kr, wp):
    m = ckv.shape[0]
    tm = 512
    assert m % tm == 0

    def tok(width):
        return pl.BlockSpec((tm, width), lambda i: (i, 0))

    return pl.pallas_call(
        _kvup_kernel, grid=(m // tm,),
        in_specs=[tok(KV_LORA), tok(LANES), _const_spec((KV_LORA, D_MODEL)), _const_spec((KV_LORA, D_MODEL)),
                  _const_spec((8, LANES)), _const_spec((2, 2 * LANES, 2 * LANES))],
        out_specs=[tok(D_MODEL), tok(D_MODEL)],
        out_shape=[jax.ShapeDtypeStruct((m, D_MODEL), BF16), jax.ShapeDtypeStruct((m, D_MODEL), BF16)],
        compiler_params=pltpu.CompilerParams(dimension_semantics=("parallel",), vmem_limit_bytes=VMEM_LIMIT),
        name="kv_up",
    )(ckv, kr, wp["w_uk"], wp["w_uv"], wp["gvecs"], wp["bd"])


def _attn_kernel(lamv_ref, gsub_ref, q_ref, k_ref, v_ref, o_ref, qt_sc, m_sc, l_sc, acc_sc, st0_sc, st1_sc,
                 *, maps, tq, tk, q_off, lam_init, bounded):
    qi = pl.program_id(1)
    rows = maps * tq
    n_full = (q_off + qi * tq) // tk
    even_chunks = (q_off // tk) % 2 == 0 and (tq // tk) % 2 == 0

    eye = (lax.broadcasted_iota(jnp.int32, (LANES, LANES), 0)
           == lax.broadcasted_iota(jnp.int32, (LANES, LANES), 1)).astype(BF16)
    for h in range(N_HEADS):
        sl = slice(h * LANES, (h + 1) * LANES)
        qt = lax.dot_general(eye, q_ref[:, sl], (((1,), (1,)), ((), ())), preferred_element_type=F32).astype(BF16)
        if maps == 2:
            sub = lax.broadcasted_iota(jnp.int32, (LANES, tq), 0)
            zero = jnp.zeros_like(qt)
            qt = jnp.concatenate([jnp.where(sub < D_B, qt, zero), jnp.where(sub >= D_B, qt, zero)], axis=1)
        qt_sc[h] = qt
        m_sc[h] = jnp.full((1, rows), -jnp.inf, F32)
        l_sc[h] = jnp.zeros((1, rows), F32)
        acc_sc[h] = jnp.zeros((LANES, rows), F32)

    def scores(h, koff, buf):
        buf[h] = _dot(k_ref[pl.ds(koff, tk), h * LANES:(h + 1) * LANES], qt_sc[h])

    def softmax_pv(h, koff, buf, vis):
        sl = slice(h * LANES, (h + 1) * LANES)
        st = buf[h]
        if vis is not None:
            st = jnp.where(vis, st, -jnp.inf)
        if bounded:
            pt = jnp.exp2(st)
            l_sc[h] = l_sc[h] + jnp.sum(pt, axis=0, keepdims=True)
            acc_sc[h] = acc_sc[h] + lax.dot_general(v_ref[pl.ds(koff, tk), sl], pt.astype(BF16),
                                                    (((0,), (0,)), ((), ())), preferred_element_type=F32)
            return
        m_prev = m_sc[h]
        m_next = jnp.maximum(m_prev, jnp.max(st, axis=0, keepdims=True))
        alpha = jnp.exp2(m_prev - m_next)
        pt = jnp.exp2(st - m_next)
        l_sc[h] = alpha * l_sc[h] + jnp.sum(pt, axis=0, keepdims=True)
        pv = lax.dot_general(v_ref[pl.ds(koff, tk), sl], pt.astype(BF16), (((0,), (0,)), ((), ())),
                             preferred_element_type=F32)
        acc_sc[h] = alpha * acc_sc[h] + pv
        m_sc[h] = m_next

    for h in range(N_HEADS):
        scores(h, 0, st0_sc)

    def step(c, cur, nxt, vis=None, last=False):
        koff = pl.multiple_of(c * tk, tk)
        knext = pl.multiple_of((c + 1) * tk, tk)
        if not last:
            scores(0, knext, nxt)
        for h in range(N_HEADS):
            if h + 1 < N_HEADS and not last:
                scores(h + 1, knext, nxt)
            softmax_pv(h, koff, cur, vis)

    def chunk_pair(i, carry):
        step(2 * i, st0_sc, st1_sc)
        step(2 * i + 1, st1_sc, st0_sc)
        return carry

    lax.fori_loop(0, n_full // 2, chunk_pair, 0)

    n_diag = tq // tk
    qc = lax.broadcasted_iota(jnp.int32, (tk, rows), 1)
    if maps == 2:
        qc = jnp.where(qc >= tq, qc - tq, qc)
    krow = lax.broadcasted_iota(jnp.int32, (tk, rows), 0)

    def diagonal(first_buf, second_buf):
        bufs = (first_buf, second_buf)
        for d in range(n_diag):
            vis = ((krow + d * tk) >> 6) <= (qc >> 6)
            step(n_full + d, bufs[d % 2], bufs[(d + 1) % 2], vis=vis, last=(d == n_diag - 1))

    if even_chunks:
        diagonal(st0_sc, st1_sc)
    else:
        odd = n_full % 2 == 1

        @pl.when(odd)
        def _odd():
            step(n_full - 1, st0_sc, st1_sc)
            diagonal(st1_sc, st0_sc)

        @pl.when(jnp.logical_not(odd))
        def _even():
            diagonal(st0_sc, st1_sc)

    if maps == 2:
        lv = lamv_ref[...]
        lam = (jnp.exp(jnp.sum(lv[0:1] * lv[1:2], axis=-1, keepdims=True))
               - jnp.exp(jnp.sum(lv[2:3] * lv[3:4], axis=-1, keepdims=True)) + lam_init)
    tb = min(tq, 2 * LANES)
    eye_q = (lax.broadcasted_iota(jnp.int32, (tb, tb), 0)
             == lax.broadcasted_iota(jnp.int32, (tb, tb), 1)).astype(BF16)
    for h in range(N_HEADS):
        sl = slice(h * LANES, (h + 1) * LANES)
        ot = acc_sc[h] / l_sc[h]
        if maps == 2:
            ot = ot[:, :tq] - lam * ot[:, tq:]
            ms = jnp.mean(ot * ot, axis=0, keepdims=True)
            ot = ((ot * lax.rsqrt(ms + EPS)) * gsub_ref[...]) * (1.0 - lam_init)
        ot = ot.astype(BF16)
        for j in range(tq // tb):
            o = lax.dot_general(eye_q, ot[:, j * tb:(j + 1) * tb], (((1,), (1,)), ((), ())),
                                preferred_element_type=F32)
            o_ref[j * tb:(j + 1) * tb, sl] = o.astype(o_ref.dtype)


SCORE_BOUND_MAX = 40.0


def _attention(q, k, v, lamv, gsub, *, maps, q_off, lam_init, tq, tk, bounded=False):
    b, sq, _ = q.shape
    sk = k.shape[1]
    assert sq % tq == 0 and tq % tk == 0 and q_off % tk == 0 and tk % CHUNK == 0 and sk == q_off + sq
    rows = maps * tq
    kern = functools.partial(_attn_kernel, maps=maps, tq=tq, tk=tk, q_off=q_off, lam_init=lam_init,
                             bounded=bounded)
    kv_spec = pl.BlockSpec((None, sk, D_MODEL), lambda bi, qi: (bi, 0, 0), pipeline_mode=pl.Buffered(1))
    return pl.pallas_call(
        kern, grid=(b, sq // tq),
        in_specs=[_const_spec((4, LANES)), _const_spec((LANES, 1)),
                  pl.BlockSpec((None, tq, D_MODEL), lambda bi, qi: (bi, qi, 0)), kv_spec, kv_spec],
        out_specs=pl.BlockSpec((None, tq, D_MODEL), lambda bi, qi: (bi, qi, 0)),
        out_shape=jax.ShapeDtypeStruct((b, sq, D_MODEL), BF16),
        scratch_shapes=[pltpu.VMEM((N_HEADS, LANES, rows), BF16), pltpu.VMEM((N_HEADS, 1, rows), F32),
                        pltpu.VMEM((N_HEADS, 1, rows), F32), pltpu.VMEM((N_HEADS, LANES, rows), F32),
                        pltpu.VMEM((N_HEADS, tk, rows), F32), pltpu.VMEM((N_HEADS, tk, rows), F32)],
        compiler_params=pltpu.CompilerParams(dimension_semantics=("parallel", "arbitrary"),
                                             vmem_limit_bytes=VMEM_LIMIT),
        name=("attn_diff" if maps == 2 else "attn_mla") + ("_bounded" if bounded else ""),
    )(lamv, gsub, q, k, v)


def _merge_kernel(x_ref, oa_ref, ob_ref, ga_ref, gb_ref, wout_ref, gffn_ref, wrh_ref, wrl_ref, br_ref,
                  h_ref, hn_ref, route_ref, counts_ref, cnt_sc):
    merged = (ga_ref[...].astype(F32) * oa_ref[...].astype(F32)
              + gb_ref[...].astype(F32) * ob_ref[...].astype(F32))
    h = x_ref[...] + _dot(merged.astype(BF16), wout_ref[...])
    h_ref[...] = h
    hn = _row_rms(h, gffn_ref[...])
    hn_ref[...] = hn

    hi = hn.astype(BF16)
    lo = (hn - hi.astype(F32)).astype(BF16)
    logits = _dot(hi, wrh_ref[...]) + (_dot(lo, wrh_ref[...]) + _dot(hi, wrl_ref[...])) + br_ref[...]
    lane = lax.broadcasted_iota(jnp.int32, logits.shape, 1).astype(F32)
    neg = jnp.full_like(logits, -jnp.inf)
    big = jnp.full_like(logits, 1e9)

    def first_argmax(vals):
        vmax = jnp.max(vals, axis=1, keepdims=True)
        return vmax, jnp.min(jnp.where(vals == vmax, lane, big), axis=1, keepdims=True)

    gmask = lane < N_GROUPS
    gmax, gsel = first_argmax(jnp.where(gmask, logits, neg))
    p_top = 1.0 / jnp.sum(jnp.where(gmask, jnp.exp(logits - gmax), 0.0), axis=1, keepdims=True)
    e_lo = N_GROUPS + gsel * EXP_PER_GROUP
    emask = jnp.logical_and(lane >= e_lo, lane < e_lo + EXP_PER_GROUP)
    le = jnp.where(emask, logits, neg)
    v1, i1 = first_argmax(le)
    v2, i2 = first_argmax(jnp.where(lane == i1, neg, le))
    t = jnp.exp(v2 - v1)
    w1 = p_top / (1.0 + t)
    w2 = p_top * t / (1.0 + t)
    e1 = i1 - N_GROUPS
    e2 = i2 - N_GROUPS

    @pl.when(pl.program_id(0) == 0)
    def _zero_counts():
        cnt_sc[...] = jnp.zeros(cnt_sc.shape, F32)

    tm = logits.shape[0]
    oh1 = lane == e1
    oh2 = lane == e2
    earlier = (lax.broadcasted_iota(jnp.int32, (tm, tm), 1)
               < lax.broadcasted_iota(jnp.int32, (tm, tm), 0)).astype(BF16)
    before1 = _dot(earlier, oh1.astype(BF16))
    before2 = _dot(earlier, oh2.astype(BF16))
    base = cnt_sc[...]
    c1 = jnp.sum(oh1.astype(F32), axis=0, keepdims=True)
    c2 = jnp.sum(oh2.astype(F32), axis=0, keepdims=True)
    rank1 = jnp.sum(jnp.where(oh1, base + before1, 0.0), axis=1, keepdims=True)
    rank2 = jnp.sum(jnp.where(oh2, (base + c1) + before2, 0.0), axis=1, keepdims=True)
    total = base + (c1 + c2)
    cnt_sc[...] = total
    counts_ref[...] = total

    route = jnp.where(lane == 0, e1, 0.0)
    route = jnp.where(lane == 1, e2, route)
    route = jnp.where(lane == 2, w1, route)
    route = jnp.where(lane == 3, w2, route)
    route = jnp.where(lane == 4, rank1, route)
    route = jnp.where(lane == 5, rank2, route)
    route_ref[...] = route


def _merge(x, oa, ob, ga, gb, wp):
    n = x.shape[0]
    tm = 512 if n % 512 == 0 else 256
    assert n % tm == 0

    def tok(width):
        return pl.BlockSpec((tm, width), lambda i: (i, 0))

    return pl.pallas_call(
        _merge_kernel, grid=(n // tm,),
        in_specs=[tok(D_MODEL)] * 5 + [_const_spec((D_MODEL, D_MODEL)), _const_spec((1, D_MODEL)),
                                       _const_spec((D_MODEL, LANES)), _const_spec((D_MODEL, LANES)),
                                       _const_spec((1, LANES))],
        out_specs=[tok(D_MODEL), tok(D_MODEL), tok(LANES), pl.BlockSpec((1, LANES), lambda i: (0, 0))],
        out_shape=[jax.ShapeDtypeStruct((n, D_MODEL), F32), jax.ShapeDtypeStruct((n, D_MODEL), F32),
                   jax.ShapeDtypeStruct((n, LANES), F32), jax.ShapeDtypeStruct((1, LANES), F32)],
        scratch_shapes=[pltpu.VMEM((1, LANES), F32)],
        compiler_params=pltpu.CompilerParams(dimension_semantics=("arbitrary",), vmem_limit_bytes=VMEM_LIMIT),
        name="merge",
    )(x, oa, ob, ga, gb, wp["w_out"], wp["g_ffn"], wp["w_r_hi"], wp["w_r_lo"], wp["b_r"])


MOE_TILE = 512


def _gather_rows(idx_ref, n_rows, src_hbm, dst_ref, sem, *, unrolled):
    def start(r):
        pltpu.make_async_copy(src_hbm.at[pl.ds(idx_ref[0, 0, r], 1), :], dst_ref.at[pl.ds(r, 1), :], sem).start()

    if unrolled:
        for r in range(n_rows):
            start(r)
    else:
        def body(r, carry):
            start(r)
            return carry
        lax.fori_loop(0, n_rows, body, 0, unroll=8)


def _wait_rows(n_rows, src_hbm, dst_ref, sem):
    pltpu.make_async_copy(src_hbm.at[pl.ds(0, n_rows), :], dst_ref, sem).wait()


DISPATCH_TILE = 256


DISPATCH_SLOTS = 3


def _dispatch_kernel(pos_ref, hn_hbm, xs_hbm, buf, load_sem, scat_sem):
    i = pl.program_id(0)
    n = pl.num_programs(0)
    td = DISPATCH_TILE
    slot = i % DISPATCH_SLOTS

    def load(t, s):
        return pltpu.make_async_copy(hn_hbm.at[pl.ds(pl.multiple_of(t * td, td), td), :], buf.at[s], load_sem.at[s])

    def drain(s):
        pltpu.make_async_copy(xs_hbm.at[pl.ds(0, 2 * td), :], xs_hbm.at[pl.ds(0, 2 * td), :], scat_sem.at[s]).wait()

    @pl.when(i == 0)
    def _first_load():
        load(0, 0).start()

    @pl.when(i + 1 < n)
    def _next_load():
        nxt = (i + 1) % DISPATCH_SLOTS

        @pl.when(i >= 2)
        def _free_slot():
            drain(nxt)

        load(i + 1, nxt).start()

    load(i, slot).wait()
    for r in range(td):
        for c in range(2):
            dst = xs_hbm.at[pl.ds(pos_ref[0, 0, c * td + r], 1), :]
            pltpu.make_async_copy(buf.at[slot, pl.ds(r, 1), :], dst, scat_sem.at[slot]).start()

    @pl.when(i == n - 1)
    def _drain_all():
        @pl.when(i >= 2)
        def _():
            drain((i + 1) % DISPATCH_SLOTS)

        @pl.when(i >= 1)
        def _():
            drain((i + 2) % DISPATCH_SLOTS)

        drain(slot)


def _dispatch(hn, pos3):
    n = hn.shape[0]
    assert n % DISPATCH_TILE == 0
    return pl.pallas_call(
        _dispatch_kernel, grid=(n // DISPATCH_TILE,),
        in_specs=[pl.BlockSpec((1, 1, 2 * DISPATCH_TILE), lambda i: (i, 0, 0), memory_space=pltpu.SMEM),
                  pl.BlockSpec(memory_space=pl.ANY)],
        out_specs=pl.BlockSpec(memory_space=pl.ANY),
        out_shape=jax.ShapeDtypeStruct((2 * n, D_MODEL), F32),
        scratch_shapes=[pltpu.VMEM((DISPATCH_SLOTS, DISPATCH_TILE, D_MODEL), F32),
                        pltpu.SemaphoreType.DMA((DISPATCH_SLOTS,)), pltpu.SemaphoreType.DMA((DISPATCH_SLOTS,))],
        compiler_params=pltpu.CompilerParams(dimension_semantics=("arbitrary",)),
        name="dispatch",
    )(pos3, hn)


def _moe_kernel(vtile_ref, vexp_ref, vlo_ref, vhi_ref, vfirst_ref, xs_ref, wei_ref, weo_ref, y_ref):
    del vexp_ref
    v = pl.program_id(0)
    x = xs_ref[...].astype(BF16)
    ab = _dot(x, wei_ref[...])
    a = ab[:, :D_EXPERT]
    act = (a * _sigmoid(a)) * ab[:, D_EXPERT:]
    y = _dot(act.astype(BF16), weo_ref[...])

    @pl.when(vfirst_ref[v] == 1)
    def _first_visit():
        y_ref[...] = y

    @pl.when(vfirst_ref[v] == 0)
    def _later_visit():
        row = vtile_ref[v] * MOE_TILE + lax.broadcasted_iota(jnp.int32, (MOE_TILE, 1), 0)
        mine = jnp.logical_and(row >= vlo_ref[v], row < vhi_ref[v])
        y_ref[...] = jnp.where(mine, y, y_ref[...])


def _moe(xs, visits, wp):
    n_visits = visits[0].shape[0]
    grid_spec = pltpu.PrefetchScalarGridSpec(
        num_scalar_prefetch=5, grid=(n_visits,),
        in_specs=[pl.BlockSpec((MOE_TILE, D_MODEL), lambda v, vt, ve, lo, hi, fi: (vt[v], 0)),
                  pl.BlockSpec((None, D_MODEL, 2 * D_EXPERT), lambda v, vt, ve, lo, hi, fi: (ve[v], 0, 0)),
                  pl.BlockSpec((None, D_EXPERT, D_MODEL), lambda v, vt, ve, lo, hi, fi: (ve[v], 0, 0))],
        out_specs=pl.BlockSpec((MOE_TILE, D_MODEL), lambda v, vt, ve, lo, hi, fi: (vt[v], 0)),
    )
    return pl.pallas_call(
        _moe_kernel, grid_spec=grid_spec,
        out_shape=jax.ShapeDtypeStruct(xs.shape, F32),
        compiler_params=pltpu.CompilerParams(dimension_semantics=("arbitrary",), vmem_limit_bytes=VMEM_LIMIT),
        name="moe",
    )(*visits, xs, wp["w_exp_in"], wp["w_exp_out"])


def _routing_tables(route, counts, n_visits):
    e = route[:, 0:2].astype(jnp.int32)
    rank = route[:, 4:6].astype(jnp.int32)
    cnt = counts[0, :N_EXPERTS].astype(jnp.int32)
    end = jnp.cumsum(cnt)
    start = end - cnt
    ids = jnp.arange(N_EXPERTS, dtype=jnp.int32)
    pos = rank + jnp.sum(jnp.where(e[:, :, None] == ids[None, None, :], start[None, None, :], 0), axis=2)

    first_tile = start // MOE_TILE
    n_vis = jnp.where(cnt > 0, (end - 1) // MOE_TILE - first_tile + 1, 0)
    vis_end = jnp.cumsum(n_vis)
    vis_start = vis_end - n_vis
    v = jnp.minimum(jnp.arange(n_visits, dtype=jnp.int32), vis_end[-1] - 1)
    v_exp = jnp.sum((v[:, None] >= vis_end[None, :]).astype(jnp.int32), axis=1)
    pick = lambda table: jnp.sum(jnp.where(v_exp[:, None] == ids[None, :], table[None, :], 0), axis=1)
    v_tile = pick(first_tile) + (v - pick(vis_start))
    v_first = jnp.concatenate([jnp.ones((1,), jnp.int32), (v_tile[1:] != v_tile[:-1]).astype(jnp.int32)])
    visits = tuple(a.astype(jnp.int32) for a in (v_tile, v_exp, pick(start), pick(end), v_first))
    return pos, visits


COMB_TILE = 256


def _combine_kernel(pos_ref, posn_ref, route_ref, h_ref, pe_ref, y_hbm, gple_ref, wg_ref, wp_ref,
                    o_ref, ybuf, sem):
    i = pl.program_id(0)
    n = pl.num_programs(0)
    slot = i % 2
    rows = 2 * COMB_TILE

    @pl.when(i == 0)
    def _first():
        _gather_rows(pos_ref, rows, y_hbm, ybuf.at[0], sem.at[0], unrolled=False)

    _wait_rows(rows, y_hbm, ybuf.at[slot], sem.at[slot])
    _gather_rows(posn_ref, rows, y_hbm, ybuf.at[1 - slot], sem.at[1 - slot], unrolled=True)
    route = route_ref[...]
    h = h_ref[...] + (route[:, 2:3] * ybuf[slot, 0:COMB_TILE, :] + route[:, 3:4] * ybuf[slot, COMB_TILE:rows, :])
    gate = _sigmoid(_dot(_row_rms(h, gple_ref[...]).astype(BF16), wg_ref[...]))
    o_ref[...] = h + gate * _dot(pe_ref[...].astype(BF16), wp_ref[...])

    @pl.when(i == n - 1)
    def _drain_extra():
        _wait_rows(rows, y_hbm, ybuf.at[1 - slot], sem.at[1 - slot])


def _combine(route, h, pe, y, pos3, wp):
    n = h.shape[0]
    assert n % COMB_TILE == 0 and COMB_TILE == DISPATCH_TILE
    n_tiles = n // COMB_TILE

    def tok(width):
        return pl.BlockSpec((COMB_TILE, width), lambda i: (i, 0))

    pos_spec = pl.BlockSpec((1, 1, 2 * COMB_TILE), lambda i: (i, 0, 0), memory_space=pltpu.SMEM)
    posn_spec = pl.BlockSpec((1, 1, 2 * COMB_TILE), lambda i: (jnp.minimum(i + 1, n_tiles - 1), 0, 0),
                             memory_space=pltpu.SMEM)
    return pl.pallas_call(
        _combine_kernel, grid=(n_tiles,),
        in_specs=[pos_spec, posn_spec, tok(LANES), tok(D_MODEL), tok(D_PLE), pl.BlockSpec(memory_space=pl.ANY),
                  _const_spec((1, D_MODEL)), _const_spec((D_MODEL, D_MODEL)), _const_spec((D_PLE, D_MODEL))],
        out_specs=tok(D_MODEL),
        out_shape=jax.ShapeDtypeStruct((n, D_MODEL), F32),
        scratch_shapes=[pltpu.VMEM((2, 2 * COMB_TILE, D_MODEL), F32), pltpu.SemaphoreType.DMA((2,))],
        compiler_params=pltpu.CompilerParams(dimension_semantics=("arbitrary",), vmem_limit_bytes=VMEM_LIMIT),
        name="combine",
    )(pos3, pos3, route, h, pe, y, wp["g_ple"], wp["w_ple_gate"], wp["w_ple_proj"])


def _rope_tables(pos, rot, offsets):
    half = rot // 2
    inv = ROPE_THETA ** (-jnp.arange(half, dtype=F32) * (2.0 / rot))
    lane = jnp.arange(LANES)
    rel = jnp.full((LANES,), -1)
    for o in offsets:
        rel = jnp.where((lane >= o) & (lane < o + rot), lane - o, rel)
    first, second = (rel >= 0) & (rel < half), rel >= half
    ang = pos.astype(F32)[:, None] * inv[jnp.maximum(rel, 0) % half][None, :]
    c, s = jnp.cos(ang), jnp.sin(ang)
    return jnp.stack([jnp.where(rel >= 0, c, 1.0), jnp.where(first, -s, 0.0), jnp.where(second, s, 0.0)])


def _prep_weights(l, g_mix, w_in, g_q_lat, w_uq, g_kv_lat, w_uk, w_uv, g_mla_qn, g_mla_qr, g_mla_kn, g_mla_kr,
                  g_diff_q, g_diff_k, lambda_q1, lambda_k1, lambda_q2, lambda_k2, g_diff_sub, w_out,
                  g_ffn, w_router_grp, b_router_grp, w_router_exp, b_router_exp, w_exp_in, w_exp_out,
                  g_ple, w_ple_gate, w_ple_proj):
    offs = [0]
    for sz in IN_SIZES:
        offs.append(offs[-1] + sz)
    wi = w_in[l]
    seg = [wi[:, offs[j]:offs[j + 1]] for j in range(len(IN_SIZES))]
    w_kr_pad = jnp.pad(seg[2], ((0, 0), (0, LANES - QK_ROPE)))
    w_main = jnp.concatenate([seg[0], seg[1], w_kr_pad] + seg[3:], axis=1).astype(BF16)

    hd = QK_NOPE + QK_ROPE
    uq = w_uq[l].reshape(Q_LORA, H_A, hd)
    uq_pad = jnp.concatenate([uq[:, :, QK_NOPE:], jnp.zeros((Q_LORA, H_A, LANES - hd), F32), uq[:, :, :QK_NOPE]],
                             axis=2).reshape(Q_LORA, H_A * LANES).astype(BF16)
    uk = w_uk[l].reshape(KV_LORA, H_A, QK_NOPE)
    uk_pad = jnp.concatenate([jnp.zeros((KV_LORA, H_A, LANES - QK_NOPE), F32), uk], axis=2)
    uk_pad = uk_pad.reshape(KV_LORA, H_A * LANES).astype(BF16)

    z32 = jnp.zeros((LANES - hd,), F32)
    gvecs = jnp.stack([
        jnp.concatenate([g_mla_qr[l], z32, g_mla_qn[l]]),
        jnp.concatenate([g_mla_kr[l], jnp.zeros((LANES - QK_ROPE,), F32)]),
        jnp.concatenate([g_diff_q[l], g_diff_q[l]]),
        jnp.concatenate([g_diff_k[l], g_diff_k[l]]),
        jnp.concatenate([jnp.zeros((LANES - QK_NOPE,), F32), g_mla_kn[l]]),
        jnp.zeros((LANES,), F32), jnp.zeros((LANES,), F32), jnp.zeros((LANES,), F32)])

    lane = jnp.arange(LANES)
    seg_a = jnp.where(lane < QK_ROPE, 0, jnp.where(lane < LANES - QK_NOPE, -1, 1))
    same_a = (seg_a[:, None] == seg_a[None, :]) & (seg_a[:, None] >= 0)
    bd_a = jnp.where(same_a, jnp.where(seg_a[:, None] == 0, 1.0 / QK_ROPE, 1.0 / QK_NOPE), 0.0)
    seg_b = lane // D_B
    bd_b = jnp.where(seg_b[:, None] == seg_b[None, :], 1.0 / D_B, 0.0)
    pair = lambda m: jnp.kron(jnp.eye(2, dtype=F32), m)
    bd = jnp.stack([pair(bd_a), pair(bd_b)]).astype(BF16)

    w_r = jnp.concatenate([w_router_grp[l], w_router_exp[l],
                           jnp.zeros((D_MODEL, LANES - N_GROUPS - N_EXPERTS), F32)], axis=1)
    w_r_hi = w_r.astype(BF16)
    w_r_lo = (w_r - w_r_hi.astype(F32)).astype(BF16)
    b_r = jnp.concatenate([b_router_grp[l], b_router_exp[l],
                           jnp.zeros((LANES - N_GROUPS - N_EXPERTS,), F32)]).reshape(1, LANES)

    amax = lambda g: jnp.max(jnp.abs(g[l]))
    bound_diff = 1.05 * DIFF_SCALE * LOG2E * D_B * amax(g_diff_q) * amax(g_diff_k)
    bound_mla = 1.05 * MLA_SCALE * LOG2E * (
        jnp.sqrt(QK_NOPE * amax(g_mla_qn) ** 2 + QK_ROPE * amax(g_mla_qr) ** 2)
        * jnp.sqrt(QK_NOPE * amax(g_mla_kn) ** 2 + QK_ROPE * amax(g_mla_kr) ** 2))

    pad64 = jnp.zeros((LANES - D_B,), F32)
    lamv = jnp.stack([jnp.concatenate([v[l], pad64]) for v in (lambda_q1, lambda_k1, lambda_q2, lambda_k2)])

    return {
        "g_mix": g_mix[l].reshape(1, D_MODEL), "w_main": w_main,
        "g_q_lat": g_q_lat[l].reshape(1, Q_LORA), "w_uq": uq_pad,
        "g_kv_lat": g_kv_lat[l].reshape(1, KV_LORA), "w_uk": uk_pad, "w_uv": w_uv[l].astype(BF16),
        "gvecs": gvecs, "bd": bd, "lamv": lamv, "bound_mla": bound_mla, "bound_diff": bound_diff, "g_diff_sub": g_diff_sub[l].reshape(V_B, 1),
        "w_out": w_out[l].astype(BF16), "g_ffn": g_ffn[l].reshape(1, D_MODEL),
        "w_r_hi": w_r_hi, "w_r_lo": w_r_lo, "b_r": b_r,
        "w_exp_in": w_exp_in[l].astype(BF16), "w_exp_out": w_exp_out[l].astype(BF16),
        "g_ple": g_ple[l].reshape(1, D_MODEL), "w_ple_gate": w_ple_gate[l].astype(BF16),
        "w_ple_proj": w_ple_proj[l].astype(BF16),
    }


def _layer(x, pe, pos, past, wp, lam_init):
    b, s, _ = x.shape
    tab_a = _rope_tables(pos, QK_ROPE, (0,))
    tab_b = _rope_tables(pos, ROT_B, (0, D_B))
    q_cat, ckv, kr, dq, dk32, dk16, dv32, dv16, ga, gb = _proj(x, tab_a, tab_b, wp)

    if past is None:
        ckv_all, kr_all, dk_all, dv_all = ckv, kr, dk16, dv16
        q_off = 0
        tq = tk = min(256, s)
    else:
        ckv_p, kr_p, dk_p, dv_p = past
        past_len = ckv_p.shape[1]
        ckv_all = jnp.concatenate([ckv_p, ckv], axis=1)
        kr_all = jnp.concatenate([jnp.pad(kr_p, ((0, 0), (0, 0), (0, LANES - QK_ROPE))), kr], axis=1)
        dk_all = jnp.concatenate([dk_p.reshape(b, past_len, D_MODEL).astype(BF16), dk16], axis=1)
        dv_all = jnp.concatenate([dv_p.reshape(b, past_len, D_MODEL).astype(BF16), dv16], axis=1)
        q_off = past_len
        tq = tk = s
    sk = ckv_all.shape[1]

    k_cat, v_a = _kv_up(ckv_all.reshape(b * sk, KV_LORA), kr_all.reshape(b * sk, LANES), wp)

    def attend(q, k, v, maps, score_bound):
        run = functools.partial(_attention, maps=maps, q_off=q_off, lam_init=lam_init, tq=tq, tk=tk)
        args = (q, k, v, wp["lamv"], wp["g_diff_sub"])
        if past is not None:
            return run(*args)
        return lax.cond(score_bound <= SCORE_BOUND_MAX,
                        lambda *a: run(*a, bounded=True), lambda *a: run(*a, bounded=False), *args)

    o_a = attend(q_cat, k_cat.reshape(b, sk, D_MODEL), v_a.reshape(b, sk, D_MODEL), 1, wp["bound_mla"])
    o_b = attend(dq, dk_all, dv_all, 2, wp["bound_diff"])

    n = b * s
    flat = lambda a: a.reshape(n, a.shape[-1])
    h, hn, route, counts = _merge(flat(x), flat(o_a), flat(o_b), flat(ga), flat(gb), wp)

    slot_of, visits = _routing_tables(route, counts, (2 * n) // MOE_TILE + N_EXPERTS - 1)
    slots3 = slot_of.reshape(n // COMB_TILE, COMB_TILE, 2).transpose(0, 2, 1).reshape(n // COMB_TILE, 1, 2 * COMB_TILE)
    y = _moe(_dispatch(hn, slots3), visits, wp)
    out = _combine(route, h, flat(pe), y, slots3, wp)

    return (out.reshape(b, s, D_MODEL),
            (ckv, kr[:, :, :QK_ROPE], dk32.reshape(b, s, H_B, 2, D_B), dv32.reshape(b, s, H_B, V_B)))


def kernel(x_prompt, x_sample, p_prompt, p_sample, cache_mla_ckv, cache_mla_krope, cache_diff_k, cache_diff_v,
           g_mix, w_in, g_q_lat, w_uq, g_kv_lat, w_uk, w_uv, g_mla_qn, g_mla_qr, g_mla_kn, g_mla_kr,
           g_diff_q, g_diff_k, lambda_q1, lambda_k1, lambda_q2, lambda_k2, g_diff_sub, w_out,
           g_ffn, w_router_grp, b_router_grp, w_router_exp, b_router_exp, w_exp_in, w_exp_out,
           g_ple, w_ple_gate, w_ple_proj):
    depth = w_in.shape[0]
    pos_p = jnp.arange(x_prompt.shape[1], dtype=jnp.int32)
    pos_s = cache_mla_ckv.shape[2] + jnp.arange(x_sample.shape[1], dtype=jnp.int32)
    hp, hs = x_prompt, x_sample
    st_p, st_s = [], []
    for l in range(depth):
        wp = _prep_weights(l, g_mix, w_in, g_q_lat, w_uq, g_kv_lat, w_uk, w_uv, g_mla_qn, g_mla_qr, g_mla_kn,
                           g_mla_kr, g_diff_q, g_diff_k, lambda_q1, lambda_k1, lambda_q2, lambda_k2, g_diff_sub,
                           w_out, g_ffn, w_router_grp, b_router_grp, w_router_exp, b_router_exp, w_exp_in,
                           w_exp_out, g_ple, w_ple_gate, w_ple_proj)
        lam_init = 0.8 - 0.6 * math.exp(-0.3 * l)
        hp, sp = _layer(hp, p_prompt[l], pos_p, None, wp, lam_init)
        hs, ss = _layer(hs, p_sample[l], pos_s,
                        (cache_mla_ckv[l], cache_mla_krope[l], cache_diff_k[l], cache_diff_v[l]), wp, lam_init)
        st_p.append(sp)
        st_s.append(ss)
    stack = lambda sts, j: jnp.stack([st[j] for st in sts])
    return (hp, hs,
            stack(st_p, 0), stack(st_p, 1), stack(st_p, 2), stack(st_p, 3),
            stack(st_s, 0), stack(st_s, 1), stack(st_s, 2), stack(st_s, 3))
```

```python
import functools
import math

import jax
import jax.numpy as jnp
from jax import lax
from jax.experimental import pallas as pl
from jax.experimental.pallas import tpu as pltpu

F32 = jnp.float32
BF16 = jnp.bfloat16

D_MODEL = 1024
CHUNK = 64
ROPE_THETA = 500000.0
EPS = 1e-6
H_A = 8
Q_LORA = 384
KV_LORA = 256
QK_NOPE = 64
QK_ROPE = 32
V_A = 128
MLA_SCALE = 1.0 / math.sqrt(QK_NOPE + QK_ROPE)
H_B = 8
D_B = 64
V_B = 2 * D_B
ROT_B = D_B // 4
DIFF_SCALE = 1.0 / math.sqrt(D_B)
LOG2E = math.log2(math.e)
N_GROUPS = 4
EXP_PER_GROUP = 8
N_EXPERTS = N_GROUPS * EXP_PER_GROUP
D_EXPERT = 256
D_PLE = 256
IN_SIZES = (Q_LORA, KV_LORA, QK_ROPE, H_B * 2 * D_B, H_B * 2 * D_B, H_B * V_B, D_MODEL, D_MODEL)

LANES = 128
N_HEADS = 8
VMEM_LIMIT = 56 * 1024 * 1024

_O_Q = 0
_O_CKV = _O_Q + Q_LORA
_O_KR = _O_CKV + KV_LORA
_O_DQ = _O_KR + LANES
_O_DK = _O_DQ + D_MODEL
_O_DV = _O_DK + D_MODEL
_O_GA = _O_DV + D_MODEL
_O_GB = _O_GA + D_MODEL
_W_MAIN_COLS = _O_GB + D_MODEL


def _dot(a, b):
    return jnp.dot(a, b, preferred_element_type=F32)


def _sigmoid(x):
    return 1.0 / (1.0 + jnp.exp(-x))


def _row_rms(x, g):
    ms = jnp.mean(x * x, axis=-1, keepdims=True)
    return (x * lax.rsqrt(ms + EPS)) * g


def _seg_rms(z, bd, g):
    ms = _dot((z * z).astype(BF16), bd)
    return (z * lax.rsqrt(ms + EPS)) * g


def _seg_rms_heads(z, bd2, g):
    g2 = jnp.concatenate([g, g], axis=1)
    blocks = []
    for p in range(z.shape[1] // (2 * LANES)):
        y = _seg_rms(z[:, p * 2 * LANES:(p + 1) * 2 * LANES], bd2, g2)
        blocks += [y[:, :LANES], y[:, LANES:]]
    return blocks


def _rope_block(y, tab_ref, half):
    return (y * tab_ref[0]
            + pltpu.roll(y, LANES - half, 1) * tab_ref[1]
            + pltpu.roll(y, half, 1) * tab_ref[2])


def _proj_kernel(x_ref, taba_ref, tabb_ref, gmix_ref, w_ref, gql_ref, wuq_ref, gkv_ref, gv_ref, bd_ref,
                 qcat_ref, ckv_ref, kr_ref, dq_ref, dk32_ref, dk16_ref, dv32_ref, dv16_ref, ga_ref, gb_ref,
                 krope_ref):
    xn = _row_rms(x_ref[...], gmix_ref[...]).astype(BF16)
    bd_a = bd_ref[0]
    bd_b = bd_ref[1]

    ql = _row_rms(_dot(xn, w_ref[:, _O_Q:_O_Q + Q_LORA]), gql_ref[...]).astype(BF16)
    q_heads = _seg_rms_heads(_dot(ql, wuq_ref[...]), bd_a, gv_ref[0:1, :])
    for h in range(N_HEADS):
        sl = slice(h * LANES, (h + 1) * LANES)
        qb = _rope_block(q_heads[h], taba_ref, QK_ROPE // 2)
        qcat_ref[:, sl] = (qb * (MLA_SCALE * LOG2E)).astype(BF16)

    ckv_ref[...] = _row_rms(_dot(xn, w_ref[:, _O_CKV:_O_CKV + KV_LORA]), gkv_ref[...])
    kr = _dot(xn, w_ref[:, _O_KR:_O_KR + LANES])
    kr = _rope_block(_seg_rms(kr, bd_ref[0, :LANES, :LANES], gv_ref[1:2, :]), taba_ref, QK_ROPE // 2)
    kr_ref[...] = kr
    krope_ref[...] = kr[:, :QK_ROPE]

    zq_heads = _seg_rms_heads(_dot(xn, w_ref[:, _O_DQ:_O_DQ + D_MODEL]), bd_b, gv_ref[2:3, :])
    zk_heads = _seg_rms_heads(_dot(xn, w_ref[:, _O_DK:_O_DK + D_MODEL]), bd_b, gv_ref[3:4, :])
    for h in range(N_HEADS):
        sl = slice(h * LANES, (h + 1) * LANES)
        qb = _rope_block(zq_heads[h], tabb_ref, ROT_B // 2)
        dq_ref[:, sl] = (qb * (DIFF_SCALE * LOG2E)).astype(BF16)
        kb = _rope_block(zk_heads[h], tabb_ref, ROT_B // 2)
        dk32_ref[:, sl] = kb
        dk16_ref[:, sl] = kb.astype(BF16)

    dv = _dot(xn, w_ref[:, _O_DV:_O_DV + D_MODEL])
    dv32_ref[...] = dv
    dv16_ref[...] = dv.astype(BF16)
    ga_ref[...] = _sigmoid(_dot(xn, w_ref[:, _O_GA:_O_GA + D_MODEL])).astype(BF16)
    gb_ref[...] = _sigmoid(_dot(xn, w_ref[:, _O_GB:_O_GB + D_MODEL])).astype(BF16)


def _const_spec(shape):
    nd = len(shape)
    return pl.BlockSpec(shape, lambda *_: (0,) * nd, pipeline_mode=pl.Buffered(1))


def _proj(x, tab_a, tab_b, wp):
    b, s, _ = x.shape
    tm = min(512, s)
    grid = (b, s // tm)

    def tok(width):
        return pl.BlockSpec((None, tm, width), lambda bi, i: (bi, i, 0))

    tab_spec = pl.BlockSpec((3, tm, LANES), lambda bi, i: (0, i, 0))
    out_shapes = [
        jax.ShapeDtypeStruct((b, s, D_MODEL), BF16),
        jax.ShapeDtypeStruct((b, s, KV_LORA), F32),
        jax.ShapeDtypeStruct((b, s, LANES), F32),
        jax.ShapeDtypeStruct((b, s, D_MODEL), BF16),
        jax.ShapeDtypeStruct((b, s, D_MODEL), F32),
        jax.ShapeDtypeStruct((b, s, D_MODEL), BF16),
        jax.ShapeDtypeStruct((b, s, D_MODEL), F32),
        jax.ShapeDtypeStruct((b, s, D_MODEL), BF16),
        jax.ShapeDtypeStruct((b, s, D_MODEL), BF16),
        jax.ShapeDtypeStruct((b, s, D_MODEL), BF16),
        jax.ShapeDtypeStruct((b, s, QK_ROPE), F32),
    ]
    out_specs = [tok(D_MODEL), tok(KV_LORA), tok(LANES), tok(D_MODEL), tok(D_MODEL), tok(D_MODEL),
                 tok(D_MODEL), tok(D_MODEL), tok(D_MODEL), tok(D_MODEL), tok(QK_ROPE)]
    in_specs = [
        tok(D_MODEL), tab_spec, tab_spec,
        _const_spec((1, D_MODEL)), _const_spec((D_MODEL, _W_MAIN_COLS)),
        _const_spec((1, Q_LORA)), _const_spec((Q_LORA, D_MODEL)), _const_spec((1, KV_LORA)),
        _const_spec((8, LANES)), _const_spec((2, 2 * LANES, 2 * LANES)),
    ]
    return pl.pallas_call(
        _proj_kernel, grid=grid, in_specs=in_specs, out_specs=out_specs, out_shape=out_shapes,
        compiler_params=pltpu.CompilerParams(dimension_semantics=("parallel", "parallel"),
                                             vmem_limit_bytes=VMEM_LIMIT),
        name="proj",
    )(x, tab_a, tab_b, wp["g_mix"], wp["w_main"], wp["g_q_lat"], wp["w_uq"], wp["g_kv_lat"],
      wp["gvecs"], wp["bd"])


def _kvup_kernel(ckv_ref, kr_ref, wuk_ref, wuv_ref, gv_ref, bd_ref, kcat_ref, va_ref):
    c = ckv_ref[...].astype(BF16)
    kn = _dot(c, wuk_ref[...])
    kr = kr_ref[...]
    kn_heads = _seg_rms_heads(kn, bd_ref[0], gv_ref[4:5, :])
    for h in range(N_HEADS):
        sl = slice(h * LANES, (h + 1) * LANES)
        kcat_ref[:, sl] = (kn_heads[h] + kr).astype(BF16)
    va_ref[...] = _dot(c, wuv_ref[...]).astype(BF16)


def _kv_up(ckv, kr, wp):
    m = ckv.shape[0]
    tm = 512
    assert m % tm == 0

    def tok(width):
        return pl.BlockSpec((tm, width), lambda i: (i, 0))

    return pl.pallas_call(
        _kvup_kernel, grid=(m // tm,),
        in_specs=[tok(KV_LORA), tok(LANES), _const_spec((KV_LORA, D_MODEL)), _const_spec((KV_LORA, D_MODEL)),
                  _const_spec((8, LANES)), _const_spec((2, 2 * LANES, 2 * LANES))],
        out_specs=[tok(D_MODEL), tok(D_MODEL)],
        out_shape=[jax.ShapeDtypeStruct((m, D_MODEL), BF16), jax.ShapeDtypeStruct((m, D_MODEL), BF16)],
        compiler_params=pltpu.CompilerParams(dimension_semantics=("parallel",), vmem_limit_bytes=VMEM_LIMIT),
        name="kv_up",
    )(ckv, kr, wp["w_uk"], wp["w_uv"], wp["gvecs"], wp["bd"])


def _attn_kernel(lamv_ref, gsub_ref, q_ref, k_ref, v_ref, o_ref, qt_sc, m_sc, l_sc, acc_sc, st0_sc, st1_sc,
                 *, maps, tq, tk, q_off, lam_init, bounded):
    qi = pl.program_id(1)
    rows = maps * tq
    n_full = (q_off + qi * tq) // tk
    even_chunks = (q_off // tk) % 2 == 0 and (tq // tk) % 2 == 0

    eye = (lax.broadcasted_iota(jnp.int32, (LANES, LANES), 0)
           == lax.broadcasted_iota(jnp.int32, (LANES, LANES), 1)).astype(BF16)
    for h in range(N_HEADS):
        sl = slice(h * LANES, (h + 1) * LANES)
        qt = lax.dot_general(eye, q_ref[:, sl], (((1,), (1,)), ((), ())), preferred_element_type=F32).astype(BF16)
        if maps == 2:
            sub = lax.broadcasted_iota(jnp.int32, (LANES, tq), 0)
            zero = jnp.zeros_like(qt)
            qt = jnp.concatenate([jnp.where(sub < D_B, qt, zero), jnp.where(sub >= D_B, qt, zero)], axis=1)
        qt_sc[h] = qt
        m_sc[h] = jnp.full((1, rows), -jnp.inf, F32)
        l_sc[h] = jnp.zeros((1, rows), F32)
        acc_sc[h] = jnp.zeros((LANES, rows), F32)

    def scores(h, koff, buf, nk=tk):
        buf[h] = _dot(k_ref[pl.ds(koff, nk), h * LANES:(h + 1) * LANES], qt_sc[h])

    def softmax_pv(h, koff, buf, vis, nk=tk):
        sl = slice(h * LANES, (h + 1) * LANES)
        st = buf[h]
        if vis is not None:
            st = jnp.where(vis, st, -jnp.inf)
        if bounded:
            pt = jnp.exp2(st)
            l_sc[h] = l_sc[h] + jnp.sum(pt, axis=0, keepdims=True)
            acc_sc[h] = acc_sc[h] + lax.dot_general(v_ref[pl.ds(koff, nk), sl], pt.astype(BF16),
                                                    (((0,), (0,)), ((), ())), preferred_element_type=F32)
            return
        m_prev = m_sc[h]
        m_next = jnp.maximum(m_prev, jnp.max(st, axis=0, keepdims=True))
        alpha = jnp.exp2(m_prev - m_next)
        pt = jnp.exp2(st - m_next)
        l_sc[h] = alpha * l_sc[h] + jnp.sum(pt, axis=0, keepdims=True)
        pv = lax.dot_general(v_ref[pl.ds(koff, nk), sl], pt.astype(BF16), (((0,), (0,)), ((), ())),
                             preferred_element_type=F32)
        acc_sc[h] = alpha * acc_sc[h] + pv
        m_sc[h] = m_next

    for h in range(N_HEADS):
        scores(h, 0, st0_sc)

    def step(koff, cur, nxt, *, nk=tk, vis=None, knext=None, next_nk=tk):
        if knext is not None:
            scores(0, knext, nxt, next_nk)
        for h in range(N_HEADS):
            if h + 1 < N_HEADS and knext is not None:
                scores(h + 1, knext, nxt, next_nk)
            softmax_pv(h, koff, cur, vis, nk)

    at = lambda c: pl.multiple_of(c * tk, tk)

    def chunk_pair(i, carry):
        step(at(2 * i), st0_sc, st1_sc, knext=at(2 * i + 1))
        step(at(2 * i + 1), st1_sc, st0_sc, knext=at(2 * i + 2))
        return carry

    dk = min(tk, tq)
    n_diag = tq // dk
    qc = lax.broadcasted_iota(jnp.int32, (dk, rows), 1)
    if maps == 2:
        qc = jnp.where(qc >= tq, qc - tq, qc)
    krow = lax.broadcasted_iota(jnp.int32, (dk, rows), 0)
    kdiag = at(n_full)

    def diagonal(first_buf, second_buf):
        bufs = (first_buf, second_buf)
        for d in range(n_diag):
            vis = ((krow + d * dk) >> 6) <= (qc >> 6)
            knext = pl.multiple_of(kdiag + (d + 1) * dk, dk) if d + 1 < n_diag else None
            step(pl.multiple_of(kdiag + d * dk, dk), bufs[d % 2], bufs[(d + 1) % 2], nk=dk, vis=vis,
                 knext=knext, next_nk=dk)

    if dk != tk:
        step(0, st0_sc, st1_sc, knext=kdiag, next_nk=dk)
        diagonal(st1_sc, st0_sc)
    elif even_chunks:
        lax.fori_loop(0, n_full // 2, chunk_pair, 0)
        diagonal(st0_sc, st1_sc)
    else:
        lax.fori_loop(0, n_full // 2, chunk_pair, 0)
        odd = n_full % 2 == 1

        @pl.when(odd)
        def _odd():
            step(at(n_full - 1), st0_sc, st1_sc, knext=kdiag)
            diagonal(st1_sc, st0_sc)

        @pl.when(jnp.logical_not(odd))
        def _even():
            diagonal(st0_sc, st1_sc)

    if maps == 2:
        lv = lamv_ref[...]
        lam = (jnp.exp(jnp.sum(lv[0:1] * lv[1:2], axis=-1, keepdims=True))
               - jnp.exp(jnp.sum(lv[2:3] * lv[3:4], axis=-1, keepdims=True)) + lam_init)
    tb = min(tq, 2 * LANES)
    eye_q = (lax.broadcasted_iota(jnp.int32, (tb, tb), 0)
             == lax.broadcasted_iota(jnp.int32, (tb, tb), 1)).astype(BF16)
    for h in range(N_HEADS):
        sl = slice(h * LANES, (h + 1) * LANES)
        ot = acc_sc[h] * (1.0 / l_sc[h])
        if maps == 2:
            ot = ot[:, :tq] - lam * ot[:, tq:]
            ms = jnp.mean(ot * ot, axis=0, keepdims=True)
            ot = ((ot * lax.rsqrt(ms + EPS)) * gsub_ref[...]) * (1.0 - lam_init)
        ot = ot.astype(BF16)
        for j in range(tq // tb):
            o = lax.dot_general(eye_q, ot[:, j * tb:(j + 1) * tb], (((1,), (1,)), ((), ())),
                                preferred_element_type=F32)
            o_ref[j * tb:(j + 1) * tb, sl] = o.astype(o_ref.dtype)


SCORE_BOUND_MAX = 40.0


def _attention(q, k, v, lamv, gsub, *, maps, q_off, lam_init, tq, tk, bounded=False):
    b, sq, _ = q.shape
    sk = k.shape[1]
    assert sq % tq == 0 and tq % CHUNK == 0 and tk % CHUNK == 0 and sk == q_off + sq
    if tk > tq:
        assert q_off == tk and sq == tq
    else:
        assert tq % tk == 0 and q_off % tk == 0
    rows = maps * tq
    kern = functools.partial(_attn_kernel, maps=maps, tq=tq, tk=tk, q_off=q_off, lam_init=lam_init,
                             bounded=bounded)
    kv_spec = pl.BlockSpec((None, sk, D_MODEL), lambda bi, qi: (bi, 0, 0))
    return pl.pallas_call(
        kern, grid=(b, sq // tq),
        in_specs=[_const_spec((4, LANES)), _const_spec((LANES, 1)),
                  pl.BlockSpec((None, tq, D_MODEL), lambda bi, qi: (bi, qi, 0)), kv_spec, kv_spec],
        out_specs=pl.BlockSpec((None, tq, D_MODEL), lambda bi, qi: (bi, qi, 0)),
        out_shape=jax.ShapeDtypeStruct((b, sq, D_MODEL), BF16),
        scratch_shapes=[pltpu.VMEM((N_HEADS, LANES, rows), BF16), pltpu.VMEM((N_HEADS, 1, rows), F32),
                        pltpu.VMEM((N_HEADS, 1, rows), F32), pltpu.VMEM((N_HEADS, LANES, rows), F32),
                        pltpu.VMEM((N_HEADS, tk, rows), F32), pltpu.VMEM((N_HEADS, min(tk, tq), rows), F32)],
        compiler_params=pltpu.CompilerParams(dimension_semantics=("parallel", "arbitrary"),
                                             vmem_limit_bytes=VMEM_LIMIT),
        name=("attn_diff" if maps == 2 else "attn_mla") + ("_bounded" if bounded else ""),
    )(lamv, gsub, q, k, v)


def _merge_kernel(x_ref, oa_ref, ob_ref, ga_ref, gb_ref, wout_ref, gffn_ref, wrh_ref, wrl_ref, br_ref,
                  h_ref, hn_ref, route_ref, counts_ref, cnt_sc):
    merged = (ga_ref[...].astype(F32) * oa_ref[...].astype(F32)
              + gb_ref[...].astype(F32) * ob_ref[...].astype(F32))
    h = x_ref[...] + _dot(merged.astype(BF16), wout_ref[...])
    h_ref[...] = h
    hn = _row_rms(h, gffn_ref[...])
    hn_ref[...] = hn

    hi = hn.astype(BF16)
    lo = (hn - hi.astype(F32)).astype(BF16)
    logits = _dot(hi, wrh_ref[...]) + (_dot(lo, wrh_ref[...]) + _dot(hi, wrl_ref[...])) + br_ref[...]
    lane = lax.broadcasted_iota(jnp.int32, logits.shape, 1).astype(F32)
    neg = jnp.full_like(logits, -jnp.inf)
    big = jnp.full_like(logits, 1e9)

    def first_argmax(vals):
        vmax = jnp.max(vals, axis=1, keepdims=True)
        return vmax, jnp.min(jnp.where(vals == vmax, lane, big), axis=1, keepdims=True)

    gmask = lane < N_GROUPS
    gmax, gsel = first_argmax(jnp.where(gmask, logits, neg))
    p_top = 1.0 / jnp.sum(jnp.where(gmask, jnp.exp(logits - gmax), 0.0), axis=1, keepdims=True)
    e_lo = N_GROUPS + gsel * EXP_PER_GROUP
    emask = jnp.logical_and(lane >= e_lo, lane < e_lo + EXP_PER_GROUP)
    le = jnp.where(emask, logits, neg)
    v1, i1 = first_argmax(le)
    v2, i2 = first_argmax(jnp.where(lane == i1, neg, le))
    t = jnp.exp(v2 - v1)
    w1 = p_top / (1.0 + t)
    w2 = p_top * t / (1.0 + t)
    e1 = i1 - N_GROUPS
    e2 = i2 - N_GROUPS

    @pl.when(pl.program_id(0) == 0)
    def _zero_counts():
        cnt_sc[...] = jnp.zeros(cnt_sc.shape, F32)

    tm = logits.shape[0]
    oh1 = lane == e1
    oh2 = lane == e2
    earlier = (lax.broadcasted_iota(jnp.int32, (tm, tm), 1)
               < lax.broadcasted_iota(jnp.int32, (tm, tm), 0)).astype(BF16)
    before1 = _dot(earlier, oh1.astype(BF16))
    before2 = _dot(earlier, oh2.astype(BF16))
    base = cnt_sc[...]
    c1 = jnp.sum(oh1.astype(F32), axis=0, keepdims=True)
    c2 = jnp.sum(oh2.astype(F32), axis=0, keepdims=True)
    rank1 = jnp.sum(jnp.where(oh1, base + before1, 0.0), axis=1, keepdims=True)
    rank2 = jnp.sum(jnp.where(oh2, (base + c1) + before2, 0.0), axis=1, keepdims=True)
    total = base + (c1 + c2)
    cnt_sc[...] = total
    counts_ref[...] = total

    route = jnp.where(lane == 0, e1, 0.0)
    route = jnp.where(lane == 1, e2, route)
    route = jnp.where(lane == 2, w1, route)
    route = jnp.where(lane == 3, w2, route)
    route = jnp.where(lane == 4, rank1, route)
    route = jnp.where(lane == 5, rank2, route)
    route_ref[...] = route


def _merge(x, oa, ob, ga, gb, wp):
    n = x.shape[0]
    tm = 512 if n % 512 == 0 else 256
    assert n % tm == 0

    def tok(width):
        return pl.BlockSpec((tm, width), lambda i: (i, 0))

    return pl.pallas_call(
        _merge_kernel, grid=(n // tm,),
        in_specs=[tok(D_MODEL)] * 5 + [_const_spec((D_MODEL, D_MODEL)), _const_spec((1, D_MODEL)),
                                       _const_spec((D_MODEL, LANES)), _const_spec((D_MODEL, LANES)),
                                       _const_spec((1, LANES))],
        out_specs=[tok(D_MODEL), tok(D_MODEL), tok(LANES), pl.BlockSpec((1, LANES), lambda i: (0, 0))],
        out_shape=[jax.ShapeDtypeStruct((n, D_MODEL), F32), jax.ShapeDtypeStruct((n, D_MODEL), F32),
                   jax.ShapeDtypeStruct((n, LANES), F32), jax.ShapeDtypeStruct((1, LANES), F32)],
        scratch_shapes=[pltpu.VMEM((1, LANES), F32)],
        compiler_params=pltpu.CompilerParams(dimension_semantics=("arbitrary",), vmem_limit_bytes=VMEM_LIMIT),
        name="merge",
    )(x, oa, ob, ga, gb, wp["w_out"], wp["g_ffn"], wp["w_r_hi"], wp["w_r_lo"], wp["b_r"])


MOE_TILE = 512


def _gather_rows(idx_ref, n_rows, src_hbm, dst_ref, sem, *, unrolled):
    def start(r):
        pltpu.make_async_copy(src_hbm.at[pl.ds(idx_ref[0, 0, r], 1), :], dst_ref.at[pl.ds(r, 1), :], sem).start()

    if unrolled:
        for r in range(n_rows):
            start(r)
    else:
        def body(r, carry):
            start(r)
            return carry
        lax.fori_loop(0, n_rows, body, 0, unroll=8)


def _wait_rows(n_rows, src_hbm, dst_ref, sem):
    pltpu.make_async_copy(src_hbm.at[pl.ds(0, n_rows), :], dst_ref, sem).wait()


DISPATCH_TILE = 256


DISPATCH_SLOTS = 3


def _dispatch_kernel(pos_ref, hn_hbm, xs_hbm, buf, load_sem, scat_sem):
    i = pl.program_id(0)
    n = pl.num_programs(0)
    td = DISPATCH_TILE
    slot = i % DISPATCH_SLOTS

    def load(t, s):
        return pltpu.make_async_copy(hn_hbm.at[pl.ds(pl.multiple_of(t * td, td), td), :], buf.at[s], load_sem.at[s])

    def drain(s):
        pltpu.make_async_copy(xs_hbm.at[pl.ds(0, 2 * td), :], xs_hbm.at[pl.ds(0, 2 * td), :], scat_sem.at[s]).wait()

    @pl.when(i == 0)
    def _first_load():
        load(0, 0).start()

    @pl.when(i + 1 < n)
    def _next_load():
        nxt = (i + 1) % DISPATCH_SLOTS

        @pl.when(i >= 2)
        def _free_slot():
            drain(nxt)

        load(i + 1, nxt).start()

    load(i, slot).wait()
    for r in range(td):
        for c in range(2):
            dst = xs_hbm.at[pl.ds(pos_ref[0, 0, c * td + r], 1), :]
            pltpu.make_async_copy(buf.at[slot, pl.ds(r, 1), :], dst, scat_sem.at[slot]).start()

    @pl.when(i == n - 1)
    def _drain_all():
        @pl.when(i >= 2)
        def _():
            drain((i + 1) % DISPATCH_SLOTS)

        @pl.when(i >= 1)
        def _():
            drain((i + 2) % DISPATCH_SLOTS)

        drain(slot)


def _dispatch(hn, pos3):
    n = hn.shape[0]
    assert n % DISPATCH_TILE == 0
    return pl.pallas_call(
        _dispatch_kernel, grid=(n // DISPATCH_TILE,),
        in_specs=[pl.BlockSpec((1, 1, 2 * DISPATCH_TILE), lambda i: (i, 0, 0), memory_space=pltpu.SMEM),
                  pl.BlockSpec(memory_space=pl.ANY)],
        out_specs=pl.BlockSpec(memory_space=pl.ANY),
        out_shape=jax.ShapeDtypeStruct((2 * n, D_MODEL), F32),
        scratch_shapes=[pltpu.VMEM((DISPATCH_SLOTS, DISPATCH_TILE, D_MODEL), F32),
                        pltpu.SemaphoreType.DMA((DISPATCH_SLOTS,)), pltpu.SemaphoreType.DMA((DISPATCH_SLOTS,))],
        compiler_params=pltpu.CompilerParams(dimension_semantics=("arbitrary",)),
        name="dispatch",
    )(pos3, hn)


def _moe_kernel(vtile_ref, vexp_ref, vlo_ref, vhi_ref, vfirst_ref, xs_ref, wei_ref, weo_ref, y_ref):
    del vexp_ref
    v = pl.program_id(0)
    x = xs_ref[...].astype(BF16)
    ab = _dot(x, wei_ref[...])
    a = ab[:, :D_EXPERT]
    act = (a * _sigmoid(a)) * ab[:, D_EXPERT:]
    y = _dot(act.astype(BF16), weo_ref[...])

    @pl.when(vfirst_ref[v] == 1)
    def _first_visit():
        y_ref[...] = y

    @pl.when(vfirst_ref[v] == 0)
    def _later_visit():
        row = vtile_ref[v] * MOE_TILE + lax.broadcasted_iota(jnp.int32, (MOE_TILE, 1), 0)
        mine = jnp.logical_and(row >= vlo_ref[v], row < vhi_ref[v])
        y_ref[...] = jnp.where(mine, y, y_ref[...])


def _moe(xs, visits, wp):
    n_visits = visits[0].shape[0]
    grid_spec = pltpu.PrefetchScalarGridSpec(
        num_scalar_prefetch=5, grid=(n_visits,),
        in_specs=[pl.BlockSpec((MOE_TILE, D_MODEL), lambda v, vt, ve, lo, hi, fi: (vt[v], 0)),
                  pl.BlockSpec((None, D_MODEL, 2 * D_EXPERT), lambda v, vt, ve, lo, hi, fi: (ve[v], 0, 0)),
                  pl.BlockSpec((None, D_EXPERT, D_MODEL), lambda v, vt, ve, lo, hi, fi: (ve[v], 0, 0))],
        out_specs=pl.BlockSpec((MOE_TILE, D_MODEL), lambda v, vt, ve, lo, hi, fi: (vt[v], 0)),
    )
    return pl.pallas_call(
        _moe_kernel, grid_spec=grid_spec,
        out_shape=jax.ShapeDtypeStruct(xs.shape, F32),
        compiler_params=pltpu.CompilerParams(dimension_semantics=("arbitrary",), vmem_limit_bytes=VMEM_LIMIT),
        name="moe",
    )(*visits, xs, wp["w_exp_in"], wp["w_exp_out"])


def _routing_tables(route, counts, n_visits):
    e = route[:, 0:2].astype(jnp.int32)
    rank = route[:, 4:6].astype(jnp.int32)
    cnt = counts[0, :N_EXPERTS].astype(jnp.int32)
    end = jnp.cumsum(cnt)
    start = end - cnt
    ids = jnp.arange(N_EXPERTS, dtype=jnp.int32)
    pos = rank + jnp.sum(jnp.where(e[:, :, None] == ids[None, None, :], start[None, None, :], 0), axis=2)

    first_tile = start // MOE_TILE
    n_vis = jnp.where(cnt > 0, (end - 1) // MOE_TILE - first_tile + 1, 0)
    vis_end = jnp.cumsum(n_vis)
    vis_start = vis_end - n_vis
    v = jnp.minimum(jnp.arange(n_visits, dtype=jnp.int32), vis_end[-1] - 1)
    v_exp = jnp.sum((v[:, None] >= vis_end[None, :]).astype(jnp.int32), axis=1)
    pick = lambda table: jnp.sum(jnp.where(v_exp[:, None] == ids[None, :], table[None, :], 0), axis=1)
    v_tile = pick(first_tile) + (v - pick(vis_start))
    v_first = jnp.concatenate([jnp.ones((1,), jnp.int32), (v_tile[1:] != v_tile[:-1]).astype(jnp.int32)])
    visits = tuple(a.astype(jnp.int32) for a in (v_tile, v_exp, pick(start), pick(end), v_first))
    return pos, visits


COMB_TILE = 256


def _combine_kernel(pos_ref, posn_ref, route_ref, h_ref, pe_ref, y_hbm, gple_ref, wg_ref, wp_ref,
                    o_ref, ybuf, sem):
    i = pl.program_id(0)
    n = pl.num_programs(0)
    slot = i % 2
    rows = 2 * COMB_TILE

    @pl.when(i == 0)
    def _first():
        _gather_rows(pos_ref, rows, y_hbm, ybuf.at[0], sem.at[0], unrolled=False)

    _wait_rows(rows, y_hbm, ybuf.at[slot], sem.at[slot])
    _gather_rows(posn_ref, rows, y_hbm, ybuf.at[1 - slot], sem.at[1 - slot], unrolled=True)
    route = route_ref[...]
    h = h_ref[...] + (route[:, 2:3] * ybuf[slot, 0:COMB_TILE, :] + route[:, 3:4] * ybuf[slot, COMB_TILE:rows, :])
    gate = _sigmoid(_dot(_row_rms(h, gple_ref[...]).astype(BF16), wg_ref[...]))
    o_ref[...] = h + gate * _dot(pe_ref[...].astype(BF16), wp_ref[...])

    @pl.when(i == n - 1)
    def _drain_extra():
        _wait_rows(rows, y_hbm, ybuf.at[1 - slot], sem.at[1 - slot])


def _combine(route, h, pe, y, pos3, wp):
    n = h.shape[0]
    assert n % COMB_TILE == 0 and COMB_TILE == DISPATCH_TILE
    n_tiles = n // COMB_TILE

    def tok(width):
        return pl.BlockSpec((COMB_TILE, width), lambda i: (i, 0))

    pos_spec = pl.BlockSpec((1, 1, 2 * COMB_TILE), lambda i: (i, 0, 0), memory_space=pltpu.SMEM)
    posn_spec = pl.BlockSpec((1, 1, 2 * COMB_TILE), lambda i: (jnp.minimum(i + 1, n_tiles - 1), 0, 0),
                             memory_space=pltpu.SMEM)
    return pl.pallas_call(
        _combine_kernel, grid=(n_tiles,),
        in_specs=[pos_spec, posn_spec, tok(LANES), tok(D_MODEL), tok(D_PLE), pl.BlockSpec(memory_space=pl.ANY),
                  _const_spec((1, D_MODEL)), _const_spec((D_MODEL, D_MODEL)), _const_spec((D_PLE, D_MODEL))],
        out_specs=tok(D_MODEL),
        out_shape=jax.ShapeDtypeStruct((n, D_MODEL), F32),
        scratch_shapes=[pltpu.VMEM((2, 2 * COMB_TILE, D_MODEL), F32), pltpu.SemaphoreType.DMA((2,))],
        compiler_params=pltpu.CompilerParams(dimension_semantics=("arbitrary",), vmem_limit_bytes=VMEM_LIMIT),
        name="combine",
    )(pos3, pos3, route, h, pe, y, wp["g_ple"], wp["w_ple_gate"], wp["w_ple_proj"])


def _rope_tables(pos, rot, offsets):
    half = rot // 2
    inv = ROPE_THETA ** (-jnp.arange(half, dtype=F32) * (2.0 / rot))
    lane = jnp.arange(LANES)
    rel = jnp.full((LANES,), -1)
    for o in offsets:
        rel = jnp.where((lane >= o) & (lane < o + rot), lane - o, rel)
    first, second = (rel >= 0) & (rel < half), rel >= half
    ang = pos.astype(F32)[:, None] * inv[jnp.maximum(rel, 0) % half][None, :]
    c, s = jnp.cos(ang), jnp.sin(ang)
    return jnp.stack([jnp.where(rel >= 0, c, 1.0), jnp.where(first, -s, 0.0), jnp.where(second, s, 0.0)])


def _prep_weights(l, g_mix, w_in, g_q_lat, w_uq, g_kv_lat, w_uk, w_uv, g_mla_qn, g_mla_qr, g_mla_kn, g_mla_kr,
                  g_diff_q, g_diff_k, lambda_q1, lambda_k1, lambda_q2, lambda_k2, g_diff_sub, w_out,
                  g_ffn, w_router_grp, b_router_grp, w_router_exp, b_router_exp, w_exp_in, w_exp_out,
                  g_ple, w_ple_gate, w_ple_proj):
    offs = [0]
    for sz in IN_SIZES:
        offs.append(offs[-1] + sz)
    wi = w_in[l]
    seg = [wi[:, offs[j]:offs[j + 1]] for j in range(len(IN_SIZES))]
    w_kr_pad = jnp.pad(seg[2], ((0, 0), (0, LANES - QK_ROPE)))
    w_main = jnp.concatenate([seg[0], seg[1], w_kr_pad] + seg[3:], axis=1).astype(BF16)

    hd = QK_NOPE + QK_ROPE
    uq = w_uq[l].reshape(Q_LORA, H_A, hd)
    uq_pad = jnp.concatenate([uq[:, :, QK_NOPE:], jnp.zeros((Q_LORA, H_A, LANES - hd), F32), uq[:, :, :QK_NOPE]],
                             axis=2).reshape(Q_LORA, H_A * LANES).astype(BF16)
    uk = w_uk[l].reshape(KV_LORA, H_A, QK_NOPE)
    uk_pad = jnp.concatenate([jnp.zeros((KV_LORA, H_A, LANES - QK_NOPE), F32), uk], axis=2)
    uk_pad = uk_pad.reshape(KV_LORA, H_A * LANES).astype(BF16)

    z32 = jnp.zeros((LANES - hd,), F32)
    gvecs = jnp.stack([
        jnp.concatenate([g_mla_qr[l], z32, g_mla_qn[l]]),
        jnp.concatenate([g_mla_kr[l], jnp.zeros((LANES - QK_ROPE,), F32)]),
        jnp.concatenate([g_diff_q[l], g_diff_q[l]]),
        jnp.concatenate([g_diff_k[l], g_diff_k[l]]),
        jnp.concatenate([jnp.zeros((LANES - QK_NOPE,), F32), g_mla_kn[l]]),
        jnp.zeros((LANES,), F32), jnp.zeros((LANES,), F32), jnp.zeros((LANES,), F32)])

    lane = jnp.arange(LANES)
    seg_a = jnp.where(lane < QK_ROPE, 0, jnp.where(lane < LANES - QK_NOPE, -1, 1))
    same_a = (seg_a[:, None] == seg_a[None, :]) & (seg_a[:, None] >= 0)
    bd_a = jnp.where(same_a, jnp.where(seg_a[:, None] == 0, 1.0 / QK_ROPE, 1.0 / QK_NOPE), 0.0)
    seg_b = lane // D_B
    bd_b = jnp.where(seg_b[:, None] == seg_b[None, :], 1.0 / D_B, 0.0)
    pair = lambda m: jnp.kron(jnp.eye(2, dtype=F32), m)
    bd = jnp.stack([pair(bd_a), pair(bd_b)]).astype(BF16)

    w_r = jnp.concatenate([w_router_grp[l], w_router_exp[l],
                           jnp.zeros((D_MODEL, LANES - N_GROUPS - N_EXPERTS), F32)], axis=1)
    w_r_hi = w_r.astype(BF16)
    w_r_lo = (w_r - w_r_hi.astype(F32)).astype(BF16)
    b_r = jnp.concatenate([b_router_grp[l], b_router_exp[l],
                           jnp.zeros((LANES - N_GROUPS - N_EXPERTS,), F32)]).reshape(1, LANES)

    amax = lambda g: jnp.max(jnp.abs(g[l]))
    bound_diff = 1.05 * DIFF_SCALE * LOG2E * D_B * amax(g_diff_q) * amax(g_diff_k)
    bound_mla = 1.05 * MLA_SCALE * LOG2E * (
        jnp.sqrt(QK_NOPE * amax(g_mla_qn) ** 2 + QK_ROPE * amax(g_mla_qr) ** 2)
        * jnp.sqrt(QK_NOPE * amax(g_mla_kn) ** 2 + QK_ROPE * amax(g_mla_kr) ** 2))

    pad64 = jnp.zeros((LANES - D_B,), F32)
    lamv = jnp.stack([jnp.concatenate([v[l], pad64]) for v in (lambda_q1, lambda_k1, lambda_q2, lambda_k2)])

    return {
        "g_mix": g_mix[l].reshape(1, D_MODEL), "w_main": w_main,
        "g_q_lat": g_q_lat[l].reshape(1, Q_LORA), "w_uq": uq_pad,
        "g_kv_lat": g_kv_lat[l].reshape(1, KV_LORA), "w_uk": uk_pad, "w_uv": w_uv[l].astype(BF16),
        "gvecs": gvecs, "bd": bd, "lamv": lamv, "bound_mla": bound_mla, "bound_diff": bound_diff, "g_diff_sub": g_diff_sub[l].reshape(V_B, 1),
        "w_out": w_out[l].astype(BF16), "g_ffn": g_ffn[l].reshape(1, D_MODEL),
        "w_r_hi": w_r_hi, "w_r_lo": w_r_lo, "b_r": b_r,
        "w_exp_in": w_exp_in[l].astype(BF16), "w_exp_out": w_exp_out[l].astype(BF16),
        "g_ple": g_ple[l].reshape(1, D_MODEL), "w_ple_gate": w_ple_gate[l].astype(BF16),
        "w_ple_proj": w_ple_proj[l].astype(BF16),
    }


def _layer(x, pe, pos, past, wp, lam_init):
    b, s, _ = x.shape
    tab_a = _rope_tables(pos, QK_ROPE, (0,))
    tab_b = _rope_tables(pos, ROT_B, (0, D_B))
    q_cat, ckv, kr, dq, dk32, dk16, dv32, dv16, ga, gb, krope = _proj(x, tab_a, tab_b, wp)

    if past is None:
        ckv_all, kr_all, dk_all, dv_all = ckv, kr, dk16, dv16
        q_off = 0
        tq = tk = min(256, s)
    else:
        ckv_p, kr_p, dk_p, dv_p = past
        past_len = ckv_p.shape[1]
        ckv_all = jnp.concatenate([ckv_p, ckv], axis=1)
        kr_all = jnp.concatenate([jnp.pad(kr_p, ((0, 0), (0, 0), (0, LANES - QK_ROPE))), kr], axis=1)
        dk_all = jnp.concatenate([dk_p.reshape(b, past_len, D_MODEL).astype(BF16), dk16], axis=1)
        dv_all = jnp.concatenate([dv_p.reshape(b, past_len, D_MODEL).astype(BF16), dv16], axis=1)
        q_off = past_len
        tq, tk = s, past_len
    sk = ckv_all.shape[1]

    k_cat, v_a = _kv_up(ckv_all.reshape(b * sk, KV_LORA), kr_all.reshape(b * sk, LANES), wp)

    def attend(q, k, v, maps, score_bound):
        run = functools.partial(_attention, maps=maps, q_off=q_off, lam_init=lam_init, tq=tq, tk=tk)
        args = (q, k, v, wp["lamv"], wp["g_diff_sub"])
        if past is not None:
            return run(*args)
        return lax.cond(score_bound <= SCORE_BOUND_MAX,
                        lambda *a: run(*a, bounded=True), lambda *a: run(*a, bounded=False), *args)

    o_a = attend(q_cat, k_cat.reshape(b, sk, D_MODEL), v_a.reshape(b, sk, D_MODEL), 1, wp["bound_mla"])
    o_b = attend(dq, dk_all, dv_all, 2, wp["bound_diff"])

    n = b * s
    flat = lambda a: a.reshape(n, a.shape[-1])
    h, hn, route, counts = _merge(flat(x), flat(o_a), flat(o_b), flat(ga), flat(gb), wp)

    slot_of, visits = _routing_tables(route, counts, (2 * n) // MOE_TILE + N_EXPERTS - 1)
    slots3 = slot_of.reshape(n // COMB_TILE, COMB_TILE, 2).transpose(0, 2, 1).reshape(n // COMB_TILE, 1, 2 * COMB_TILE)
    y = _moe(_dispatch(hn, slots3), visits, wp)
    out = _combine(route, h, flat(pe), y, slots3, wp)

    return (out.reshape(b, s, D_MODEL),
            (ckv, krope, dk32.reshape(b, s, H_B, 2, D_B), dv32.reshape(b, s, H_B, V_B)))


def kernel(x_prompt, x_sample, p_prompt, p_sample, cache_mla_ckv, cache_mla_krope, cache_diff_k, cache_diff_v,
           g_mix, w_in, g_q_lat, w_uq, g_kv_lat, w_uk, w_uv, g_mla_qn, g_mla_qr, g_mla_kn, g_mla_kr,
           g_diff_q, g_diff_k, lambda_q1, lambda_k1, lambda_q2, lambda_k2, g_diff_sub, w_out,
           g_ffn, w_router_grp, b_router_grp, w_router_exp, b_router_exp, w_exp_in, w_exp_out,
           g_ple, w_ple_gate, w_ple_proj):
    depth = w_in.shape[0]
    pos_p = jnp.arange(x_prompt.shape[1], dtype=jnp.int32)
    pos_s = cache_mla_ckv.shape[2] + jnp.arange(x_sample.shape[1], dtype=jnp.int32)
    hp, hs = x_prompt, x_sample
    st_p, st_s = [], []
    for l in range(depth):
        wp = _prep_weights(l, g_mix, w_in, g_q_lat, w_uq, g_kv_lat, w_uk, w_uv, g_mla_qn, g_mla_qr, g_mla_kn,
                           g_mla_kr, g_diff_q, g_diff_k, lambda_q1, lambda_k1, lambda_q2, lambda_k2, g_diff_sub,
                           w_out, g_ffn, w_router_grp, b_router_grp, w_router_exp, b_router_exp, w_exp_in,
                           w_exp_out, g_ple, w_ple_gate, w_ple_proj)
        lam_init = 0.8 - 0.6 * math.exp(-0.3 * l)
        hp, sp = _layer(hp, p_prompt[l], pos_p, None, wp, lam_init)
        hs, ss = _layer(hs, p_sample[l], pos_s,
                        (cache_mla_ckv[l], cache_mla_krope[l], cache_diff_k[l], cache_diff_v[l]), wp, lam_init)
        st_p.append(sp)
        st_s.append(ss)
    stack = lambda sts, j: jnp.stack([st[j] for st in sts])
    return (hp, hs,
            stack(st_p, 0), stack(st_p, 1), stack(st_p, 2), stack(st_p, 3),
            stack(st_s, 0), stack(st_s, 1), stack(st_s, 2), stack(st_s, 3))
```

```python
import functools
import math

import jax
import jax.numpy as jnp
from jax import lax
from jax.experimental import pallas as pl
from jax.experimental.pallas import tpu as pltpu

F32 = jnp.float32
BF16 = jnp.bfloat16

D_MODEL = 1024
CHUNK = 64
ROPE_THETA = 500000.0
EPS = 1e-6
H_A = 8
Q_LORA = 384
KV_LORA = 256
QK_NOPE = 64
QK_ROPE = 32
V_A = 128
MLA_SCALE = 1.0 / math.sqrt(QK_NOPE + QK_ROPE)
H_B = 8
D_B = 64
V_B = 2 * D_B
ROT_B = D_B // 4
DIFF_SCALE = 1.0 / math.sqrt(D_B)
LOG2E = math.log2(math.e)
N_GROUPS = 4
EXP_PER_GROUP = 8
N_EXPERTS = N_GROUPS * EXP_PER_GROUP
D_EXPERT = 256
D_PLE = 256
IN_SIZES = (Q_LORA, KV_LORA, QK_ROPE, H_B * 2 * D_B, H_B * 2 * D_B, H_B * V_B, D_MODEL, D_MODEL)

LANES = 128
N_HEADS = 8
VMEM_LIMIT = 56 * 1024 * 1024

_O_Q = 0
_O_CKV = _O_Q + Q_LORA
_O_KR = _O_CKV + KV_LORA
_O_DQ = _O_KR + LANES
_O_DK = _O_DQ + D_MODEL
_O_DV = _O_DK + D_MODEL
_O_GA = _O_DV + D_MODEL
_O_GB = _O_GA + D_MODEL
_W_MAIN_COLS = _O_GB + D_MODEL


def _dot(a, b):
    return jnp.dot(a, b, preferred_element_type=F32)


def _sigmoid(x):
    return 1.0 / (1.0 + jnp.exp(-x))


def _row_rms(x, g):
    ms = jnp.mean(x * x, axis=-1, keepdims=True)
    return (x * lax.rsqrt(ms + EPS)) * g


def _seg_rms(z, bd, g):
    ms = _dot((z * z).astype(BF16), bd)
    return (z * lax.rsqrt(ms + EPS)) * g


def _seg_rms_heads(z, bd2, g):
    g2 = jnp.concatenate([g, g], axis=1)
    blocks = []
    for p in range(z.shape[1] // (2 * LANES)):
        y = _seg_rms(z[:, p * 2 * LANES:(p + 1) * 2 * LANES], bd2, g2)
        blocks += [y[:, :LANES], y[:, LANES:]]
    return blocks


def _rope_block(y, tab_ref, half):
    return (y * tab_ref[0]
            + pltpu.roll(y, LANES - half, 1) * tab_ref[1]
            + pltpu.roll(y, half, 1) * tab_ref[2])


def _proj_kernel(x_ref, taba_ref, tabb_ref, gmix_ref, w_ref, gql_ref, wuq_ref, gkv_ref, gv_ref, bd_ref,
                 qcat_ref, ckv_ref, kr_ref, dq_ref, dk32_ref, dk16_ref, dv32_ref, dv16_ref, ga_ref, gb_ref,
                 krope_ref):
    xn = _row_rms(x_ref[...], gmix_ref[...]).astype(BF16)
    bd_a = bd_ref[0]
    bd_b = bd_ref[1]

    ql = _row_rms(_dot(xn, w_ref[:, _O_Q:_O_Q + Q_LORA]), gql_ref[...]).astype(BF16)
    q_heads = _seg_rms_heads(_dot(ql, wuq_ref[...]), bd_a, gv_ref[0:1, :])
    for h in range(N_HEADS):
        sl = slice(h * LANES, (h + 1) * LANES)
        qb = _rope_block(q_heads[h], taba_ref, QK_ROPE // 2)
        qcat_ref[:, sl] = (qb * (MLA_SCALE * LOG2E)).astype(BF16)

    ckv_ref[...] = _row_rms(_dot(xn, w_ref[:, _O_CKV:_O_CKV + KV_LORA]), gkv_ref[...])
    kr = _dot(xn, w_ref[:, _O_KR:_O_KR + LANES])
    kr = _rope_block(_seg_rms(kr, bd_ref[0, :LANES, :LANES], gv_ref[1:2, :]), taba_ref, QK_ROPE // 2)
    kr_ref[...] = kr
    krope_ref[...] = kr[:, :QK_ROPE]

    zq_heads = _seg_rms_heads(_dot(xn, w_ref[:, _O_DQ:_O_DQ + D_MODEL]), bd_b, gv_ref[2:3, :])
    zk_heads = _seg_rms_heads(_dot(xn, w_ref[:, _O_DK:_O_DK + D_MODEL]), bd_b, gv_ref[3:4, :])
    for h in range(N_HEADS):
        sl = slice(h * LANES, (h + 1) * LANES)
        qb = _rope_block(zq_heads[h], tabb_ref, ROT_B // 2)
        dq_ref[:, sl] = (qb * (DIFF_SCALE * LOG2E)).astype(BF16)
        kb = _rope_block(zk_heads[h], tabb_ref, ROT_B // 2)
        dk32_ref[:, sl] = kb
        dk16_ref[:, sl] = kb.astype(BF16)

    dv = _dot(xn, w_ref[:, _O_DV:_O_DV + D_MODEL])
    dv32_ref[...] = dv
    dv16_ref[...] = dv.astype(BF16)
    ga_ref[...] = _sigmoid(_dot(xn, w_ref[:, _O_GA:_O_GA + D_MODEL])).astype(BF16)
    gb_ref[...] = _sigmoid(_dot(xn, w_ref[:, _O_GB:_O_GB + D_MODEL])).astype(BF16)


def _const_spec(shape):
    nd = len(shape)
    return pl.BlockSpec(shape, lambda *_: (0,) * nd, pipeline_mode=pl.Buffered(1))


def _proj(x, tab_a, tab_b, wp):
    b, s, _ = x.shape
    tm = min(512, s)
    grid = (b, s // tm)

    def tok(width):
        return pl.BlockSpec((None, tm, width), lambda bi, i: (bi, i, 0))

    tab_spec = pl.BlockSpec((3, tm, LANES), lambda bi, i: (0, i, 0))
    out_shapes = [
        jax.ShapeDtypeStruct((b, s, D_MODEL), BF16),
        jax.ShapeDtypeStruct((b, s, KV_LORA), F32),
        jax.ShapeDtypeStruct((b, s, LANES), F32),
        jax.ShapeDtypeStruct((b, s, D_MODEL), BF16),
        jax.ShapeDtypeStruct((b, s, D_MODEL), F32),
        jax.ShapeDtypeStruct((b, s, D_MODEL), BF16),
        jax.ShapeDtypeStruct((b, s, D_MODEL), F32),
        jax.ShapeDtypeStruct((b, s, D_MODEL), BF16),
        jax.ShapeDtypeStruct((b, s, D_MODEL), BF16),
        jax.ShapeDtypeStruct((b, s, D_MODEL), BF16),
        jax.ShapeDtypeStruct((b, s, QK_ROPE), F32),
    ]
    out_specs = [tok(D_MODEL), tok(KV_LORA), tok(LANES), tok(D_MODEL), tok(D_MODEL), tok(D_MODEL),
                 tok(D_MODEL), tok(D_MODEL), tok(D_MODEL), tok(D_MODEL), tok(QK_ROPE)]
    in_specs = [
        tok(D_MODEL), tab_spec, tab_spec,
        _const_spec((1, D_MODEL)), _const_spec((D_MODEL, _W_MAIN_COLS)),
        _const_spec((1, Q_LORA)), _const_spec((Q_LORA, D_MODEL)), _const_spec((1, KV_LORA)),
        _const_spec((8, LANES)), _const_spec((2, 2 * LANES, 2 * LANES)),
    ]
    return pl.pallas_call(
        _proj_kernel, grid=grid, in_specs=in_specs, out_specs=out_specs, out_shape=out_shapes,
        compiler_params=pltpu.CompilerParams(dimension_semantics=("parallel", "parallel"),
                                             vmem_limit_bytes=VMEM_LIMIT),
        name="proj",
    )(x, tab_a, tab_b, wp["g_mix"], wp["w_main"], wp["g_q_lat"], wp["w_uq"], wp["g_kv_lat"],
      wp["gvecs"], wp["bd"])


def _kvup_kernel(ckv_ref, kr_ref, wuk_ref, wuv_ref, gv_ref, bd_ref, kcat_ref, va_ref):
    c = ckv_ref[...].astype(BF16)
    kn = _dot(c, wuk_ref[...])
    kr = kr_ref[...]
    kn_heads = _seg_rms_heads(kn, bd_ref[0], gv_ref[4:5, :])
    for h in range(N_HEADS):
        sl = slice(h * LANES, (h + 1) * LANES)
        kcat_ref[:, sl] = (kn_heads[h] + kr).astype(BF16)
    va_ref[...] = _dot(c, wuv_ref[...]).astype(BF16)


def _kv_up(ckv, kr, wp):
    m = ckv.shape[0]
    tm = 512
    assert m % tm == 0

    def tok(width):
        return pl.BlockSpec((tm, width), lambda i: (i, 0))

    return pl.pallas_call(
        _kvup_kernel, grid=(m // tm,),
        in_specs=[tok(KV_LORA), tok(LANES), _const_spec((KV_LORA, D_MODEL)), _const_spec((KV_LORA, D_MODEL)),
                  _const_spec((8, LANES)), _const_spec((2, 2 * LANES, 2 * LANES))],
        out_specs=[tok(D_MODEL), tok(D_MODEL)],
        out_shape=[jax.ShapeDtypeStruct((m, D_MODEL), BF16), jax.ShapeDtypeStruct((m, D_MODEL), BF16)],
        compiler_params=pltpu.CompilerParams(dimension_semantics=("parallel",), vmem_limit_bytes=VMEM_LIMIT),
        name="kv_up",
    )(ckv, kr, wp["w_uk"], wp["w_uv"], wp["gvecs"], wp["bd"])


def _attn_kernel(lamv_ref, gsub_ref, q_ref, k_ref, v_ref, o_ref, qt_sc, m_sc, l_sc, acc_sc, st0_sc, st1_sc,
                 *, maps, tq, tk, q_off, lam_init, bounded):
    qi = pl.program_id(1)
    rows = maps * tq
    n_full = (q_off + qi * tq) // tk
    even_chunks = (q_off // tk) % 2 == 0 and (tq // tk) % 2 == 0

    eye = (lax.broadcasted_iota(jnp.int32, (LANES, LANES), 0)
           == lax.broadcasted_iota(jnp.int32, (LANES, LANES), 1)).astype(BF16)
    for h in range(N_HEADS):
        sl = slice(h * LANES, (h + 1) * LANES)
        qt = lax.dot_general(eye, q_ref[:, sl], (((1,), (1,)), ((), ())), preferred_element_type=F32).astype(BF16)
        if maps == 2:
            sub = lax.broadcasted_iota(jnp.int32, (LANES, tq), 0)
            zero = jnp.zeros_like(qt)
            qt = jnp.concatenate([jnp.where(sub < D_B, qt, zero), jnp.where(sub >= D_B, qt, zero)], axis=1)
        qt_sc[h] = qt
        m_sc[h] = jnp.full((1, rows), -jnp.inf, F32)
        l_sc[h] = jnp.zeros((1, rows), F32)
        acc_sc[h] = jnp.zeros((LANES, rows), F32)

    def scores(h, koff, buf, nk=tk):
        buf[h] = _dot(k_ref[pl.ds(koff, nk), h * LANES:(h + 1) * LANES], qt_sc[h])

    def softmax_pv(h, koff, buf, vis, nk=tk):
        sl = slice(h * LANES, (h + 1) * LANES)
        st = buf[h]
        if vis is not None:
            st = jnp.where(vis, st, -jnp.inf)
        if bounded:
            pt = jnp.exp2(st)
            l_sc[h] = l_sc[h] + jnp.sum(pt, axis=0, keepdims=True)
            acc_sc[h] = acc_sc[h] + lax.dot_general(v_ref[pl.ds(koff, nk), sl], pt.astype(BF16),
                                                    (((0,), (0,)), ((), ())), preferred_element_type=F32)
            return
        m_prev = m_sc[h]
        m_next = jnp.maximum(m_prev, jnp.max(st, axis=0, keepdims=True))
        alpha = jnp.exp2(m_prev - m_next)
        pt = jnp.exp2(st - m_next)
        l_sc[h] = alpha * l_sc[h] + jnp.sum(pt, axis=0, keepdims=True)
        pv = lax.dot_general(v_ref[pl.ds(koff, nk), sl], pt.astype(BF16), (((0,), (0,)), ((), ())),
                             preferred_element_type=F32)
        acc_sc[h] = alpha * acc_sc[h] + pv
        m_sc[h] = m_next

    for h in range(N_HEADS):
        scores(h, 0, st0_sc)

    def step(koff, cur, nxt, *, nk=tk, vis=None, knext=None, next_nk=tk):
        if knext is not None:
            scores(0, knext, nxt, next_nk)
        for h in range(N_HEADS):
            if h + 1 < N_HEADS and knext is not None:
                scores(h + 1, knext, nxt, next_nk)
            softmax_pv(h, koff, cur, vis, nk)

    at = lambda c: pl.multiple_of(c * tk, tk)

    def chunk_pair(i, carry):
        step(at(2 * i), st0_sc, st1_sc, knext=at(2 * i + 1))
        step(at(2 * i + 1), st1_sc, st0_sc, knext=at(2 * i + 2))
        return carry

    dk = min(tk, tq)
    n_diag = tq // dk
    qc = lax.broadcasted_iota(jnp.int32, (dk, rows), 1)
    if maps == 2:
        qc = jnp.where(qc >= tq, qc - tq, qc)
    krow = lax.broadcasted_iota(jnp.int32, (dk, rows), 0)
    kdiag = at(n_full)

    def diagonal(first_buf, second_buf):
        bufs = (first_buf, second_buf)
        for d in range(n_diag):
            vis = ((krow + d * dk) >> 6) <= (qc >> 6)
            knext = pl.multiple_of(kdiag + (d + 1) * dk, dk) if d + 1 < n_diag else None
            step(pl.multiple_of(kdiag + d * dk, dk), bufs[d % 2], bufs[(d + 1) % 2], nk=dk, vis=vis,
                 knext=knext, next_nk=dk)

    if dk != tk:
        step(0, st0_sc, st1_sc, knext=kdiag, next_nk=dk)
        diagonal(st1_sc, st0_sc)
    elif even_chunks:
        lax.fori_loop(0, n_full // 2, chunk_pair, 0)
        diagonal(st0_sc, st1_sc)
    else:
        lax.fori_loop(0, n_full // 2, chunk_pair, 0)
        odd = n_full % 2 == 1

        @pl.when(odd)
        def _odd():
            step(at(n_full - 1), st0_sc, st1_sc, knext=kdiag)
            diagonal(st1_sc, st0_sc)

        @pl.when(jnp.logical_not(odd))
        def _even():
            diagonal(st0_sc, st1_sc)

    if maps == 2:
        lv = lamv_ref[...]
        lam = (jnp.exp(jnp.sum(lv[0:1] * lv[1:2], axis=-1, keepdims=True))
               - jnp.exp(jnp.sum(lv[2:3] * lv[3:4], axis=-1, keepdims=True)) + lam_init)
    tb = min(tq, 2 * LANES)
    eye_q = (lax.broadcasted_iota(jnp.int32, (tb, tb), 0)
             == lax.broadcasted_iota(jnp.int32, (tb, tb), 1)).astype(BF16)
    for h in range(N_HEADS):
        sl = slice(h * LANES, (h + 1) * LANES)
        ot = acc_sc[h] * (1.0 / l_sc[h])
        if maps == 2:
            ot = ot[:, :tq] - lam * ot[:, tq:]
            ms = jnp.mean(ot * ot, axis=0, keepdims=True)
            ot = ((ot * lax.rsqrt(ms + EPS)) * gsub_ref[...]) * (1.0 - lam_init)
        ot = ot.astype(BF16)
        for j in range(tq // tb):
            o = lax.dot_general(eye_q, ot[:, j * tb:(j + 1) * tb], (((1,), (1,)), ((), ())),
                                preferred_element_type=F32)
            o_ref[j * tb:(j + 1) * tb, sl] = o.astype(o_ref.dtype)


SCORE_BOUND_MAX = 40.0


def _attention(q, k, v, lamv, gsub, *, maps, q_off, lam_init, tq, tk, bounded=False):
    b, sq, _ = q.shape
    sk = k.shape[1]
    assert sq % tq == 0 and tq % CHUNK == 0 and tk % CHUNK == 0 and sk == q_off + sq
    if tk > tq:
        assert q_off == tk and sq == tq
    else:
        assert tq % tk == 0 and q_off % tk == 0
    rows = maps * tq
    kern = functools.partial(_attn_kernel, maps=maps, tq=tq, tk=tk, q_off=q_off, lam_init=lam_init,
                             bounded=bounded)
    kv_spec = pl.BlockSpec((None, sk, D_MODEL), lambda bi, qi: (bi, 0, 0))
    return pl.pallas_call(
        kern, grid=(b, sq // tq),
        in_specs=[_const_spec((4, LANES)), _const_spec((LANES, 1)),
                  pl.BlockSpec((None, tq, D_MODEL), lambda bi, qi: (bi, qi, 0)), kv_spec, kv_spec],
        out_specs=pl.BlockSpec((None, tq, D_MODEL), lambda bi, qi: (bi, qi, 0)),
        out_shape=jax.ShapeDtypeStruct((b, sq, D_MODEL), BF16),
        scratch_shapes=[pltpu.VMEM((N_HEADS, LANES, rows), BF16), pltpu.VMEM((N_HEADS, 1, rows), F32),
                        pltpu.VMEM((N_HEADS, 1, rows), F32), pltpu.VMEM((N_HEADS, LANES, rows), F32),
                        pltpu.VMEM((N_HEADS, tk, rows), F32), pltpu.VMEM((N_HEADS, min(tk, tq), rows), F32)],
        compiler_params=pltpu.CompilerParams(dimension_semantics=("parallel", "arbitrary"),
                                             vmem_limit_bytes=VMEM_LIMIT),
        name=("attn_diff" if maps == 2 else "attn_mla") + ("_bounded" if bounded else ""),
    )(lamv, gsub, q, k, v)


def _merge_kernel(x_ref, oa_ref, ob_ref, ga_ref, gb_ref, wout_ref, gffn_ref, wrh_ref, wrl_ref, br_ref,
                  h_ref, hn_ref, route_ref, counts_ref, cnt_sc):
    merged = (ga_ref[...].astype(F32) * oa_ref[...].astype(F32)
              + gb_ref[...].astype(F32) * ob_ref[...].astype(F32))
    h = x_ref[...] + _dot(merged.astype(BF16), wout_ref[...])
    h_ref[...] = h
    hn = _row_rms(h, gffn_ref[...])
    hn_ref[...] = hn

    hi = hn.astype(BF16)
    lo = (hn - hi.astype(F32)).astype(BF16)
    logits = _dot(hi, wrh_ref[...]) + (_dot(lo, wrh_ref[...]) + _dot(hi, wrl_ref[...])) + br_ref[...]
    lane = lax.broadcasted_iota(jnp.int32, logits.shape, 1).astype(F32)
    neg = jnp.full_like(logits, -jnp.inf)
    big = jnp.full_like(logits, 1e9)

    def first_argmax(vals):
        vmax = jnp.max(vals, axis=1, keepdims=True)
        return vmax, jnp.min(jnp.where(vals == vmax, lane, big), axis=1, keepdims=True)

    gmask = lane < N_GROUPS
    gmax, gsel = first_argmax(jnp.where(gmask, logits, neg))
    p_top = 1.0 / jnp.sum(jnp.where(gmask, jnp.exp(logits - gmax), 0.0), axis=1, keepdims=True)
    e_lo = N_GROUPS + gsel * EXP_PER_GROUP
    emask = jnp.logical_and(lane >= e_lo, lane < e_lo + EXP_PER_GROUP)
    le = jnp.where(emask, logits, neg)
    v1, i1 = first_argmax(le)
    v2, i2 = first_argmax(jnp.where(lane == i1, neg, le))
    t = jnp.exp(v2 - v1)
    w1 = p_top / (1.0 + t)
    w2 = p_top * t / (1.0 + t)
    e1 = i1 - N_GROUPS
    e2 = i2 - N_GROUPS

    @pl.when(pl.program_id(0) == 0)
    def _zero_counts():
        cnt_sc[...] = jnp.zeros(cnt_sc.shape, F32)

    tm = logits.shape[0]
    oh1 = lane == e1
    oh2 = lane == e2
    earlier = (lax.broadcasted_iota(jnp.int32, (tm, tm), 1)
               < lax.broadcasted_iota(jnp.int32, (tm, tm), 0)).astype(BF16)
    before1 = _dot(earlier, oh1.astype(BF16))
    before2 = _dot(earlier, oh2.astype(BF16))
    base = cnt_sc[...]
    c1 = jnp.sum(oh1.astype(F32), axis=0, keepdims=True)
    c2 = jnp.sum(oh2.astype(F32), axis=0, keepdims=True)
    rank1 = jnp.sum(jnp.where(oh1, base + before1, 0.0), axis=1, keepdims=True)
    rank2 = jnp.sum(jnp.where(oh2, (base + c1) + before2, 0.0), axis=1, keepdims=True)
    total = base + (c1 + c2)
    cnt_sc[...] = total
    counts_ref[...] = total

    route = jnp.where(lane == 0, e1, 0.0)
    route = jnp.where(lane == 1, e2, route)
    route = jnp.where(lane == 2, w1, route)
    route = jnp.where(lane == 3, w2, route)
    route = jnp.where(lane == 4, rank1, route)
    route = jnp.where(lane == 5, rank2, route)
    route_ref[...] = route


def _merge(x, oa, ob, ga, gb, wp):
    n = x.shape[0]
    tm = 512 if n % 512 == 0 else 256
    assert n % tm == 0

    def tok(width):
        return pl.BlockSpec((tm, width), lambda i: (i, 0))

    return pl.pallas_call(
        _merge_kernel, grid=(n // tm,),
        in_specs=[tok(D_MODEL)] * 5 + [_const_spec((D_MODEL, D_MODEL)), _const_spec((1, D_MODEL)),
                                       _const_spec((D_MODEL, LANES)), _const_spec((D_MODEL, LANES)),
                                       _const_spec((1, LANES))],
        out_specs=[tok(D_MODEL), tok(D_MODEL), tok(LANES), pl.BlockSpec((1, LANES), lambda i: (0, 0))],
        out_shape=[jax.ShapeDtypeStruct((n, D_MODEL), F32), jax.ShapeDtypeStruct((n, D_MODEL), F32),
                   jax.ShapeDtypeStruct((n, LANES), F32), jax.ShapeDtypeStruct((1, LANES), F32)],
        scratch_shapes=[pltpu.VMEM((1, LANES), F32)],
        compiler_params=pltpu.CompilerParams(dimension_semantics=("arbitrary",), vmem_limit_bytes=VMEM_LIMIT),
        name="merge",
    )(x, oa, ob, ga, gb, wp["w_out"], wp["g_ffn"], wp["w_r_hi"], wp["w_r_lo"], wp["b_r"])


MOE_TILE = 512


def _gather_rows(idx_ref, n_rows, src_hbm, dst_ref, sem, *, unrolled):
    def start(r):
        pltpu.make_async_copy(src_hbm.at[pl.ds(idx_ref[0, 0, r], 1), :], dst_ref.at[pl.ds(r, 1), :], sem).start()

    if unrolled:
        for r in range(n_rows):
            start(r)
    else:
        def body(r, carry):
            start(r)
            return carry
        lax.fori_loop(0, n_rows, body, 0, unroll=8)


def _wait_rows(n_rows, src_hbm, dst_ref, sem):
    pltpu.make_async_copy(src_hbm.at[pl.ds(0, n_rows), :], dst_ref, sem).wait()


DISPATCH_TILE = 256


DISPATCH_SLOTS = 3


def _dispatch_kernel(pos_ref, hn_hbm, xs_hbm, buf, load_sem, scat_sem):
    i = pl.program_id(0)
    n = pl.num_programs(0)
    td = DISPATCH_TILE
    slot = i % DISPATCH_SLOTS

    def load(t, s):
        return pltpu.make_async_copy(hn_hbm.at[pl.ds(pl.multiple_of(t * td, td), td), :], buf.at[s], load_sem.at[s])

    def drain(s):
        pltpu.make_async_copy(xs_hbm.at[pl.ds(0, 2 * td), :], xs_hbm.at[pl.ds(0, 2 * td), :], scat_sem.at[s]).wait()

    @pl.when(i == 0)
    def _first_load():
        load(0, 0).start()

    @pl.when(i + 1 < n)
    def _next_load():
        nxt = (i + 1) % DISPATCH_SLOTS

        @pl.when(i >= 2)
        def _free_slot():
            drain(nxt)

        load(i + 1, nxt).start()

    load(i, slot).wait()
    for r in range(td):
        for c in range(2):
            dst = xs_hbm.at[pl.ds(pos_ref[0, 0, c * td + r], 1), :]
            pltpu.make_async_copy(buf.at[slot, pl.ds(r, 1), :], dst, scat_sem.at[slot]).start()

    @pl.when(i == n - 1)
    def _drain_all():
        @pl.when(i >= 2)
        def _():
            drain((i + 1) % DISPATCH_SLOTS)

        @pl.when(i >= 1)
        def _():
            drain((i + 2) % DISPATCH_SLOTS)

        drain(slot)


def _dispatch(hn, pos3):
    n = hn.shape[0]
    assert n % DISPATCH_TILE == 0
    return pl.pallas_call(
        _dispatch_kernel, grid=(n // DISPATCH_TILE,),
        in_specs=[pl.BlockSpec((1, 1, 2 * DISPATCH_TILE), lambda i: (i, 0, 0), memory_space=pltpu.SMEM),
                  pl.BlockSpec(memory_space=pl.ANY)],
        out_specs=pl.BlockSpec(memory_space=pl.ANY),
        out_shape=jax.ShapeDtypeStruct((2 * n, D_MODEL), F32),
        scratch_shapes=[pltpu.VMEM((DISPATCH_SLOTS, DISPATCH_TILE, D_MODEL), F32),
                        pltpu.SemaphoreType.DMA((DISPATCH_SLOTS,)), pltpu.SemaphoreType.DMA((DISPATCH_SLOTS,))],
        compiler_params=pltpu.CompilerParams(dimension_semantics=("arbitrary",)),
        name="dispatch",
    )(pos3, hn)


def _moe_kernel(vtile_ref, vexp_ref, vlo_ref, vhi_ref, vfirst_ref, xs_ref, wei_ref, weo_ref, y_ref):
    del vexp_ref
    v = pl.program_id(0)
    x = xs_ref[...].astype(BF16)
    ab = _dot(x, wei_ref[...])
    a = ab[:, :D_EXPERT]
    act = (a * _sigmoid(a)) * ab[:, D_EXPERT:]
    y = _dot(act.astype(BF16), weo_ref[...])

    @pl.when(vfirst_ref[v] == 1)
    def _first_visit():
        y_ref[...] = y

    @pl.when(vfirst_ref[v] == 0)
    def _later_visit():
        tile = y_ref.shape[0]
        row = vtile_ref[v] * tile + lax.broadcasted_iota(jnp.int32, (tile, 1), 0)
        mine = jnp.logical_and(row >= vlo_ref[v], row < vhi_ref[v])
        y_ref[...] = jnp.where(mine, y, y_ref[...])


def _moe(xs, visits, wp, tile):
    n_visits = visits[0].shape[0]
    grid_spec = pltpu.PrefetchScalarGridSpec(
        num_scalar_prefetch=5, grid=(n_visits,),
        in_specs=[pl.BlockSpec((tile, D_MODEL), lambda v, vt, ve, lo, hi, fi: (vt[v], 0)),
                  pl.BlockSpec((None, D_MODEL, 2 * D_EXPERT), lambda v, vt, ve, lo, hi, fi: (ve[v], 0, 0)),
                  pl.BlockSpec((None, D_EXPERT, D_MODEL), lambda v, vt, ve, lo, hi, fi: (ve[v], 0, 0))],
        out_specs=pl.BlockSpec((tile, D_MODEL), lambda v, vt, ve, lo, hi, fi: (vt[v], 0)),
    )
    return pl.pallas_call(
        _moe_kernel, grid_spec=grid_spec,
        out_shape=jax.ShapeDtypeStruct(xs.shape, F32),
        compiler_params=pltpu.CompilerParams(dimension_semantics=("arbitrary",), vmem_limit_bytes=VMEM_LIMIT),
        name="moe",
    )(*visits, xs, wp["w_exp_in"], wp["w_exp_out"])


def _routing_tables(route, counts, tile):
    e = route[:, 0:2].astype(jnp.int32)
    rank = route[:, 4:6].astype(jnp.int32)
    cnt = counts[0, :N_EXPERTS].astype(jnp.int32)
    end = jnp.cumsum(cnt)
    start = end - cnt
    ids = jnp.arange(N_EXPERTS, dtype=jnp.int32)
    pos = rank + jnp.sum(jnp.where(e[:, :, None] == ids[None, None, :], start[None, None, :], 0), axis=2)

    n_visits = (2 * route.shape[0]) // tile + N_EXPERTS - 1
    first_tile = start // tile
    n_vis = jnp.where(cnt > 0, (end - 1) // tile - first_tile + 1, 0)
    vis_end = jnp.cumsum(n_vis)
    vis_start = vis_end - n_vis
    v = jnp.minimum(jnp.arange(n_visits, dtype=jnp.int32), vis_end[-1] - 1)
    v_exp = jnp.sum((v[:, None] >= vis_end[None, :]).astype(jnp.int32), axis=1)
    pick = lambda table: jnp.sum(jnp.where(v_exp[:, None] == ids[None, :], table[None, :], 0), axis=1)
    v_tile = pick(first_tile) + (v - pick(vis_start))
    v_first = jnp.concatenate([jnp.ones((1,), jnp.int32), (v_tile[1:] != v_tile[:-1]).astype(jnp.int32)])
    visits = tuple(a.astype(jnp.int32) for a in (v_tile, v_exp, pick(start), pick(end), v_first))
    return pos, visits


COMB_TILE = 256


def _combine_kernel(pos_ref, posn_ref, route_ref, h_ref, pe_ref, y_hbm, gple_ref, wg_ref, wp_ref,
                    o_ref, ybuf, sem):
    i = pl.program_id(0)
    n = pl.num_programs(0)
    slot = i % 2
    rows = 2 * COMB_TILE

    @pl.when(i == 0)
    def _first():
        _gather_rows(pos_ref, rows, y_hbm, ybuf.at[0], sem.at[0], unrolled=False)

    _wait_rows(rows, y_hbm, ybuf.at[slot], sem.at[slot])
    _gather_rows(posn_ref, rows, y_hbm, ybuf.at[1 - slot], sem.at[1 - slot], unrolled=True)
    route = route_ref[...]
    h = h_ref[...] + (route[:, 2:3] * ybuf[slot, 0:COMB_TILE, :] + route[:, 3:4] * ybuf[slot, COMB_TILE:rows, :])
    gate = _sigmoid(_dot(_row_rms(h, gple_ref[...]).astype(BF16), wg_ref[...]))
    o_ref[...] = h + gate * _dot(pe_ref[...].astype(BF16), wp_ref[...])

    @pl.when(i == n - 1)
    def _drain_extra():
        _wait_rows(rows, y_hbm, ybuf.at[1 - slot], sem.at[1 - slot])


def _combine(route, h, pe, y, pos3, wp):
    n = h.shape[0]
    assert n % COMB_TILE == 0 and COMB_TILE == DISPATCH_TILE
    n_tiles = n // COMB_TILE

    def tok(width):
        return pl.BlockSpec((COMB_TILE, width), lambda i: (i, 0))

    pos_spec = pl.BlockSpec((1, 1, 2 * COMB_TILE), lambda i: (i, 0, 0), memory_space=pltpu.SMEM)
    posn_spec = pl.BlockSpec((1, 1, 2 * COMB_TILE), lambda i: (jnp.minimum(i + 1, n_tiles - 1), 0, 0),
                             memory_space=pltpu.SMEM)
    return pl.pallas_call(
        _combine_kernel, grid=(n_tiles,),
        in_specs=[pos_spec, posn_spec, tok(LANES), tok(D_MODEL), tok(D_PLE), pl.BlockSpec(memory_space=pl.ANY),
                  _const_spec((1, D_MODEL)), _const_spec((D_MODEL, D_MODEL)), _const_spec((D_PLE, D_MODEL))],
        out_specs=tok(D_MODEL),
        out_shape=jax.ShapeDtypeStruct((n, D_MODEL), F32),
        scratch_shapes=[pltpu.VMEM((2, 2 * COMB_TILE, D_MODEL), F32), pltpu.SemaphoreType.DMA((2,))],
        compiler_params=pltpu.CompilerParams(dimension_semantics=("arbitrary",), vmem_limit_bytes=VMEM_LIMIT),
        name="combine",
    )(pos3, pos3, route, h, pe, y, wp["g_ple"], wp["w_ple_gate"], wp["w_ple_proj"])


def _rope_tables(pos, rot, offsets):
    half = rot // 2
    inv = ROPE_THETA ** (-jnp.arange(half, dtype=F32) * (2.0 / rot))
    lane = jnp.arange(LANES)
    rel = jnp.full((LANES,), -1)
    for o in offsets:
        rel = jnp.where((lane >= o) & (lane < o + rot), lane - o, rel)
    first, second = (rel >= 0) & (rel < half), rel >= half
    ang = pos.astype(F32)[:, None] * inv[jnp.maximum(rel, 0) % half][None, :]
    c, s = jnp.cos(ang), jnp.sin(ang)
    return jnp.stack([jnp.where(rel >= 0, c, 1.0), jnp.where(first, -s, 0.0), jnp.where(second, s, 0.0)])


def _prep_weights(l, g_mix, w_in, g_q_lat, w_uq, g_kv_lat, w_uk, w_uv, g_mla_qn, g_mla_qr, g_mla_kn, g_mla_kr,
                  g_diff_q, g_diff_k, lambda_q1, lambda_k1, lambda_q2, lambda_k2, g_diff_sub, w_out,
                  g_ffn, w_router_grp, b_router_grp, w_router_exp, b_router_exp, w_exp_in, w_exp_out,
                  g_ple, w_ple_gate, w_ple_proj):
    offs = [0]
    for sz in IN_SIZES:
        offs.append(offs[-1] + sz)
    wi = w_in[l]
    seg = [wi[:, offs[j]:offs[j + 1]] for j in range(len(IN_SIZES))]
    w_kr_pad = jnp.pad(seg[2], ((0, 0), (0, LANES - QK_ROPE)))
    w_main = jnp.concatenate([seg[0], seg[1], w_kr_pad] + seg[3:], axis=1).astype(BF16)

    hd = QK_NOPE + QK_ROPE
    uq = w_uq[l].reshape(Q_LORA, H_A, hd)
    uq_pad = jnp.concatenate([uq[:, :, QK_NOPE:], jnp.zeros((Q_LORA, H_A, LANES - hd), F32), uq[:, :, :QK_NOPE]],
                             axis=2).reshape(Q_LORA, H_A * LANES).astype(BF16)
    uk = w_uk[l].reshape(KV_LORA, H_A, QK_NOPE)
    uk_pad = jnp.concatenate([jnp.zeros((KV_LORA, H_A, LANES - QK_NOPE), F32), uk], axis=2)
    uk_pad = uk_pad.reshape(KV_LORA, H_A * LANES).astype(BF16)

    z32 = jnp.zeros((LANES - hd,), F32)
    gvecs = jnp.stack([
        jnp.concatenate([g_mla_qr[l], z32, g_mla_qn[l]]),
        jnp.concatenate([g_mla_kr[l], jnp.zeros((LANES - QK_ROPE,), F32)]),
        jnp.concatenate([g_diff_q[l], g_diff_q[l]]),
        jnp.concatenate([g_diff_k[l], g_diff_k[l]]),
        jnp.concatenate([jnp.zeros((LANES - QK_NOPE,), F32), g_mla_kn[l]]),
        jnp.zeros((LANES,), F32), jnp.zeros((LANES,), F32), jnp.zeros((LANES,), F32)])

    lane = jnp.arange(LANES)
    seg_a = jnp.where(lane < QK_ROPE, 0, jnp.where(lane < LANES - QK_NOPE, -1, 1))
    same_a = (seg_a[:, None] == seg_a[None, :]) & (seg_a[:, None] >= 0)
    bd_a = jnp.where(same_a, jnp.where(seg_a[:, None] == 0, 1.0 / QK_ROPE, 1.0 / QK_NOPE), 0.0)
    seg_b = lane // D_B
    bd_b = jnp.where(seg_b[:, None] == seg_b[None, :], 1.0 / D_B, 0.0)
    pair = lambda m: jnp.kron(jnp.eye(2, dtype=F32), m)
    bd = jnp.stack([pair(bd_a), pair(bd_b)]).astype(BF16)

    w_r = jnp.concatenate([w_router_grp[l], w_router_exp[l],
                           jnp.zeros((D_MODEL, LANES - N_GROUPS - N_EXPERTS), F32)], axis=1)
    w_r_hi = w_r.astype(BF16)
    w_r_lo = (w_r - w_r_hi.astype(F32)).astype(BF16)
    b_r = jnp.concatenate([b_router_grp[l], b_router_exp[l],
                           jnp.zeros((LANES - N_GROUPS - N_EXPERTS,), F32)]).reshape(1, LANES)

    amax = lambda g: jnp.max(jnp.abs(g[l]))
    bound_diff = 1.05 * DIFF_SCALE * LOG2E * D_B * amax(g_diff_q) * amax(g_diff_k)
    bound_mla = 1.05 * MLA_SCALE * LOG2E * (
        jnp.sqrt(QK_NOPE * amax(g_mla_qn) ** 2 + QK_ROPE * amax(g_mla_qr) ** 2)
        * jnp.sqrt(QK_NOPE * amax(g_mla_kn) ** 2 + QK_ROPE * amax(g_mla_kr) ** 2))

    pad64 = jnp.zeros((LANES - D_B,), F32)
    lamv = jnp.stack([jnp.concatenate([v[l], pad64]) for v in (lambda_q1, lambda_k1, lambda_q2, lambda_k2)])

    return {
        "g_mix": g_mix[l].reshape(1, D_MODEL), "w_main": w_main,
        "g_q_lat": g_q_lat[l].reshape(1, Q_LORA), "w_uq": uq_pad,
        "g_kv_lat": g_kv_lat[l].reshape(1, KV_LORA), "w_uk": uk_pad, "w_uv": w_uv[l].astype(BF16),
        "gvecs": gvecs, "bd": bd, "lamv": lamv, "bound_mla": bound_mla, "bound_diff": bound_diff, "g_diff_sub": g_diff_sub[l].reshape(V_B, 1),
        "w_out": w_out[l].astype(BF16), "g_ffn": g_ffn[l].reshape(1, D_MODEL),
        "w_r_hi": w_r_hi, "w_r_lo": w_r_lo, "b_r": b_r,
        "w_exp_in": w_exp_in[l].astype(BF16), "w_exp_out": w_exp_out[l].astype(BF16),
        "g_ple": g_ple[l].reshape(1, D_MODEL), "w_ple_gate": w_ple_gate[l].astype(BF16),
        "w_ple_proj": w_ple_proj[l].astype(BF16),
    }


def _layer(x, pe, pos, past, wp, lam_init):
    b, s, _ = x.shape
    tab_a = _rope_tables(pos, QK_ROPE, (0,))
    tab_b = _rope_tables(pos, ROT_B, (0, D_B))
    if s < 512:
        outs = _proj(x.reshape(1, b * s, D_MODEL), jnp.tile(tab_a, (1, b, 1)), jnp.tile(tab_b, (1, b, 1)), wp)
        outs = [o.reshape(b, s, o.shape[-1]) for o in outs]
    else:
        outs = _proj(x, tab_a, tab_b, wp)
    q_cat, ckv, kr, dq, dk32, dk16, dv32, dv16, ga, gb, krope = outs

    if past is None:
        ckv_all, kr_all, dk_all, dv_all = ckv, kr, dk16, dv16
        q_off = 0
        tq = tk = min(256, s)
    else:
        ckv_p, kr_p, dk_p, dv_p = past
        past_len = ckv_p.shape[1]
        ckv_all = jnp.concatenate([ckv_p, ckv], axis=1)
        kr_all = jnp.concatenate([jnp.pad(kr_p, ((0, 0), (0, 0), (0, LANES - QK_ROPE))), kr], axis=1)
        dk_all = jnp.concatenate([dk_p.reshape(b, past_len, D_MODEL).astype(BF16), dk16], axis=1)
        dv_all = jnp.concatenate([dv_p.reshape(b, past_len, D_MODEL).astype(BF16), dv16], axis=1)
        q_off = past_len
        tq, tk = s, past_len
    sk = ckv_all.shape[1]

    k_cat, v_a = _kv_up(ckv_all.reshape(b * sk, KV_LORA), kr_all.reshape(b * sk, LANES), wp)

    def attend(q, k, v, maps, score_bound):
        run = functools.partial(_attention, maps=maps, q_off=q_off, lam_init=lam_init, tq=tq, tk=tk)
        args = (q, k, v, wp["lamv"], wp["g_diff_sub"])
        if past is not None:
            return run(*args)
        return lax.cond(score_bound <= SCORE_BOUND_MAX,
                        lambda *a: run(*a, bounded=True), lambda *a: run(*a, bounded=False), *args)

    o_a = attend(q_cat, k_cat.reshape(b, sk, D_MODEL), v_a.reshape(b, sk, D_MODEL), 1, wp["bound_mla"])
    o_b = attend(dq, dk_all, dv_all, 2, wp["bound_diff"])

    n = b * s
    flat = lambda a: a.reshape(n, a.shape[-1])
    h, hn, route, counts = _merge(flat(x), flat(o_a), flat(o_b), flat(ga), flat(gb), wp)

    moe_tile = MOE_TILE if 2 * n >= 16 * MOE_TILE else MOE_TILE // 4
    slot_of, visits = _routing_tables(route, counts, moe_tile)
    slots3 = slot_of.reshape(n // COMB_TILE, COMB_TILE, 2).transpose(0, 2, 1).reshape(n // COMB_TILE, 1, 2 * COMB_TILE)
    y = _moe(_dispatch(hn, slots3), visits, wp, moe_tile)
    out = _combine(route, h, flat(pe), y, slots3, wp)

    return (out.reshape(b, s, D_MODEL),
            (ckv, krope, dk32.reshape(b, s, H_B, 2, D_B), dv32.reshape(b, s, H_B, V_B)))


def kernel(x_prompt, x_sample, p_prompt, p_sample, cache_mla_ckv, cache_mla_krope, cache_diff_k, cache_diff_v,
           g_mix, w_in, g_q_lat, w_uq, g_kv_lat, w_uk, w_uv, g_mla_qn, g_mla_qr, g_mla_kn, g_mla_kr,
           g_diff_q, g_diff_k, lambda_q1, lambda_k1, lambda_q2, lambda_k2, g_diff_sub, w_out,
           g_ffn, w_router_grp, b_router_grp, w_router_exp, b_router_exp, w_exp_in, w_exp_out,
           g_ple, w_ple_gate, w_ple_proj):
    depth = w_in.shape[0]
    pos_p = jnp.arange(x_prompt.shape[1], dtype=jnp.int32)
    pos_s = cache_mla_ckv.shape[2] + jnp.arange(x_sample.shape[1], dtype=jnp.int32)
    hp, hs = x_prompt, x_sample
    st_p, st_s = [], []
    for l in range(depth):
        wp = _prep_weights(l, g_mix, w_in, g_q_lat, w_uq, g_kv_lat, w_uk, w_uv, g_mla_qn, g_mla_qr, g_mla_kn,
                           g_mla_kr, g_diff_q, g_diff_k, lambda_q1, lambda_k1, lambda_q2, lambda_k2, g_diff_sub,
                           w_out, g_ffn, w_router_grp, b_router_grp, w_router_exp, b_router_exp, w_exp_in,
                           w_exp_out, g_ple, w_ple_gate, w_ple_proj)
        lam_init = 0.8 - 0.6 * math.exp(-0.3 * l)
        hp, sp = _layer(hp, p_prompt[l], pos_p, None, wp, lam_init)
        hs, ss = _layer(hs, p_sample[l], pos_s,
                        (cache_mla_ckv[l], cache_mla_krope[l], cache_diff_k[l], cache_diff_v[l]), wp, lam_init)
        st_p.append(sp)
        st_s.append(ss)
    stack = lambda sts, j: jnp.stack([st[j] for st in sts])
    return (hp, hs,
            stack(st_p, 0), stack(st_p, 1), stack(st_p, 2), stack(st_p, 3),
            stack(st_s, 0), stack(st_s, 1), stack(st_s, 2), stack(st_s, 3))
```

```python
import functools
import math

import jax
import jax.numpy as jnp
from jax import lax
from jax.experimental import pallas as pl
from jax.experimental.pallas import tpu as pltpu

F32 = jnp.float32
BF16 = jnp.bfloat16

D_MODEL = 1024
CHUNK = 64
ROPE_THETA = 500000.0
EPS = 1e-6
H_A = 8
Q_LORA = 384
KV_LORA = 256
QK_NOPE = 64
QK_ROPE = 32
V_A = 128
MLA_SCALE = 1.0 / math.sqrt(QK_NOPE + QK_ROPE)
H_B = 8
D_B = 64
V_B = 2 * D_B
ROT_B = D_B // 4
DIFF_SCALE = 1.0 / math.sqrt(D_B)
LOG2E = math.log2(math.e)
N_GROUPS = 4
EXP_PER_GROUP = 8
N_EXPERTS = N_GROUPS * EXP_PER_GROUP
D_EXPERT = 256
D_PLE = 256
IN_SIZES = (Q_LORA, KV_LORA, QK_ROPE, H_B * 2 * D_B, H_B * 2 * D_B, H_B * V_B, D_MODEL, D_MODEL)

LANES = 128
N_HEADS = 8
VMEM_LIMIT = 56 * 1024 * 1024

_O_Q = 0
_O_CKV = _O_Q + Q_LORA
_O_KR = _O_CKV + KV_LORA
_O_DQ = _O_KR + LANES
_O_DK = _O_DQ + D_MODEL
_O_DV = _O_DK + D_MODEL
_O_GA = _O_DV + D_MODEL
_O_GB = _O_GA + D_MODEL
_W_MAIN_COLS = _O_GB + D_MODEL


def _dot(a, b):
    return jnp.dot(a, b, preferred_element_type=F32)


def _sigmoid(x):
    return 1.0 / (1.0 + jnp.exp(-x))


def _row_rms(x, g):
    ms = jnp.mean(x * x, axis=-1, keepdims=True)
    return (x * lax.rsqrt(ms + EPS)) * g


def _seg_rms(z, bd, g):
    ms = _dot((z * z).astype(BF16), bd)
    return (z * lax.rsqrt(ms + EPS)) * g


def _seg_rms_heads(z, bd2, g):
    g2 = jnp.concatenate([g, g], axis=1)
    blocks = []
    for p in range(z.shape[1] // (2 * LANES)):
        y = _seg_rms(z[:, p * 2 * LANES:(p + 1) * 2 * LANES], bd2, g2)
        blocks += [y[:, :LANES], y[:, LANES:]]
    return blocks


def _rope_block(y, tab_ref, half):
    return (y * tab_ref[0]
            + pltpu.roll(y, LANES - half, 1) * tab_ref[1]
            + pltpu.roll(y, half, 1) * tab_ref[2])


def _proj_kernel(x_ref, taba_ref, tabb_ref, gmix_ref, w_ref, gql_ref, wuq_ref, gkv_ref, gv_ref, bd_ref,
                 qcat_ref, ckv_ref, kr_ref, dq_ref, dk32_ref, dk16_ref, dv32_ref, dv16_ref, ga_ref, gb_ref,
                 krope_ref):
    xn = _row_rms(x_ref[...], gmix_ref[...]).astype(BF16)
    bd_a = bd_ref[0]
    bd_b = bd_ref[1]

    ql = _row_rms(_dot(xn, w_ref[:, _O_Q:_O_Q + Q_LORA]), gql_ref[...]).astype(BF16)
    q_heads = _seg_rms_heads(_dot(ql, wuq_ref[...]), bd_a, gv_ref[0:1, :])
    for h in range(N_HEADS):
        sl = slice(h * LANES, (h + 1) * LANES)
        qb = _rope_block(q_heads[h], taba_ref, QK_ROPE // 2)
        qcat_ref[:, sl] = (qb * (MLA_SCALE * LOG2E)).astype(BF16)

    ckv_ref[...] = _row_rms(_dot(xn, w_ref[:, _O_CKV:_O_CKV + KV_LORA]), gkv_ref[...])
    kr = _dot(xn, w_ref[:, _O_KR:_O_KR + LANES])
    kr = _rope_block(_seg_rms(kr, bd_ref[0, :LANES, :LANES], gv_ref[1:2, :]), taba_ref, QK_ROPE // 2)
    kr_ref[...] = kr
    krope_ref[...] = kr[:, :QK_ROPE]

    zq_heads = _seg_rms_heads(_dot(xn, w_ref[:, _O_DQ:_O_DQ + D_MODEL]), bd_b, gv_ref[2:3, :])
    zk_heads = _seg_rms_heads(_dot(xn, w_ref[:, _O_DK:_O_DK + D_MODEL]), bd_b, gv_ref[3:4, :])
    for h in range(N_HEADS):
        sl = slice(h * LANES, (h + 1) * LANES)
        qb = _rope_block(zq_heads[h], tabb_ref, ROT_B // 2)
        dq_ref[:, sl] = (qb * (DIFF_SCALE * LOG2E)).astype(BF16)
        kb = _rope_block(zk_heads[h], tabb_ref, ROT_B // 2)
        dk32_ref[:, sl] = kb
        dk16_ref[:, sl] = kb.astype(BF16)

    dv = _dot(xn, w_ref[:, _O_DV:_O_DV + D_MODEL])
    dv32_ref[...] = dv
    dv16_ref[...] = dv.astype(BF16)
    ga_ref[...] = _sigmoid(_dot(xn, w_ref[:, _O_GA:_O_GA + D_MODEL])).astype(BF16)
    gb_ref[...] = _sigmoid(_dot(xn, w_ref[:, _O_GB:_O_GB + D_MODEL])).astype(BF16)


def _const_spec(shape):
    nd = len(shape)
    return pl.BlockSpec(shape, lambda *_: (0,) * nd, pipeline_mode=pl.Buffered(1))


def _proj(x, tab_a, tab_b, wp):
    b, s, _ = x.shape
    tm = min(512, s)
    grid = (b, s // tm)

    def tok(width):
        return pl.BlockSpec((None, tm, width), lambda bi, i: (bi, i, 0))

    tab_spec = pl.BlockSpec((3, tm, LANES), lambda bi, i: (0, i, 0))
    out_shapes = [
        jax.ShapeDtypeStruct((b, s, D_MODEL), BF16),
        jax.ShapeDtypeStruct((b, s, KV_LORA), F32),
        jax.ShapeDtypeStruct((b, s, LANES), F32),
        jax.ShapeDtypeStruct((b, s, D_MODEL), BF16),
        jax.ShapeDtypeStruct((b, s, D_MODEL), F32),
        jax.ShapeDtypeStruct((b, s, D_MODEL), BF16),
        jax.ShapeDtypeStruct((b, s, D_MODEL), F32),
        jax.ShapeDtypeStruct((b, s, D_MODEL), BF16),
        jax.ShapeDtypeStruct((b, s, D_MODEL), BF16),
        jax.ShapeDtypeStruct((b, s, D_MODEL), BF16),
        jax.ShapeDtypeStruct((b, s, QK_ROPE), F32),
    ]
    out_specs = [tok(D_MODEL), tok(KV_LORA), tok(LANES), tok(D_MODEL), tok(D_MODEL), tok(D_MODEL),
                 tok(D_MODEL), tok(D_MODEL), tok(D_MODEL), tok(D_MODEL), tok(QK_ROPE)]
    in_specs = [
        tok(D_MODEL), tab_spec, tab_spec,
        _const_spec((1, D_MODEL)), _const_spec((D_MODEL, _W_MAIN_COLS)),
        _const_spec((1, Q_LORA)), _const_spec((Q_LORA, D_MODEL)), _const_spec((1, KV_LORA)),
        _const_spec((8, LANES)), _const_spec((2, 2 * LANES, 2 * LANES)),
    ]
    return pl.pallas_call(
        _proj_kernel, grid=grid, in_specs=in_specs, out_specs=out_specs, out_shape=out_shapes,
        compiler_params=pltpu.CompilerParams(dimension_semantics=("parallel", "parallel"),
                                             vmem_limit_bytes=VMEM_LIMIT),
        name="proj",
    )(x, tab_a, tab_b, wp["g_mix"], wp["w_main"], wp["g_q_lat"], wp["w_uq"], wp["g_kv_lat"],
      wp["gvecs"], wp["bd"])


def _kvup_kernel(ckv_ref, kr_ref, wuk_ref, wuv_ref, gv_ref, bd_ref, kcat_ref, va_ref):
    c = ckv_ref[...].astype(BF16)
    kn = _dot(c, wuk_ref[...])
    kr = kr_ref[...]
    kn_heads = _seg_rms_heads(kn, bd_ref[0], gv_ref[4:5, :])
    for h in range(N_HEADS):
        sl = slice(h * LANES, (h + 1) * LANES)
        kcat_ref[:, sl] = (kn_heads[h] + kr).astype(BF16)
    va_ref[...] = _dot(c, wuv_ref[...]).astype(BF16)


def _kv_up(ckv, kr, wp):
    m = ckv.shape[0]
    tm = 512
    assert m % tm == 0

    def tok(width):
        return pl.BlockSpec((tm, width), lambda i: (i, 0))

    return pl.pallas_call(
        _kvup_kernel, grid=(m // tm,),
        in_specs=[tok(KV_LORA), tok(LANES), _const_spec((KV_LORA, D_MODEL)), _const_spec((KV_LORA, D_MODEL)),
                  _const_spec((8, LANES)), _const_spec((2, 2 * LANES, 2 * LANES))],
        out_specs=[tok(D_MODEL), tok(D_MODEL)],
        out_shape=[jax.ShapeDtypeStruct((m, D_MODEL), BF16), jax.ShapeDtypeStruct((m, D_MODEL), BF16)],
        compiler_params=pltpu.CompilerParams(dimension_semantics=("parallel",), vmem_limit_bytes=VMEM_LIMIT),
        name="kv_up",
    )(ckv, kr, wp["w_uk"], wp["w_uv"], wp["gvecs"], wp["bd"])


def _attn_kernel(lamv_ref, gsub_ref, q_ref, k_ref, v_ref, o_ref, qt_sc, m_sc, l_sc, acc_sc, st0_sc, st1_sc,
                 *, maps, tq, tk, q_off, lam_init, bounded):
    qi = pl.program_id(1)
    rows = maps * tq
    n_full = (q_off + qi * tq) // tk
    even_chunks = (q_off // tk) % 2 == 0 and (tq // tk) % 2 == 0

    eye = (lax.broadcasted_iota(jnp.int32, (LANES, LANES), 0)
           == lax.broadcasted_iota(jnp.int32, (LANES, LANES), 1)).astype(BF16)
    for h in range(N_HEADS):
        sl = slice(h * LANES, (h + 1) * LANES)
        qt = lax.dot_general(eye, q_ref[:, sl], (((1,), (1,)), ((), ())), preferred_element_type=F32).astype(BF16)
        if maps == 2:
            sub = lax.broadcasted_iota(jnp.int32, (LANES, tq), 0)
            zero = jnp.zeros_like(qt)
            qt = jnp.concatenate([jnp.where(sub < D_B, qt, zero), jnp.where(sub >= D_B, qt, zero)], axis=1)
        qt_sc[h] = qt
        m_sc[h] = jnp.full((1, rows), -jnp.inf, F32)
        l_sc[h] = jnp.zeros((1, rows), F32)
        acc_sc[h] = jnp.zeros((LANES, rows), F32)

    def scores(h, koff, buf, nk=tk):
        buf[h] = _dot(k_ref[pl.ds(koff, nk), h * LANES:(h + 1) * LANES], qt_sc[h])

    def softmax_pv(h, koff, buf, vis, nk=tk):
        sl = slice(h * LANES, (h + 1) * LANES)
        st = buf[h]
        if vis is not None:
            st = jnp.where(vis, st, -jnp.inf)
        if bounded:
            pt = jnp.exp2(st)
            l_sc[h] = l_sc[h] + jnp.sum(pt, axis=0, keepdims=True)
            acc_sc[h] = acc_sc[h] + lax.dot_general(v_ref[pl.ds(koff, nk), sl], pt.astype(BF16),
                                                    (((0,), (0,)), ((), ())), preferred_element_type=F32)
            return
        m_prev = m_sc[h]
        m_next = jnp.maximum(m_prev, jnp.max(st, axis=0, keepdims=True))
        alpha = jnp.exp2(m_prev - m_next)
        pt = jnp.exp2(st - m_next)
        l_sc[h] = alpha * l_sc[h] + jnp.sum(pt, axis=0, keepdims=True)
        pv = lax.dot_general(v_ref[pl.ds(koff, nk), sl], pt.astype(BF16), (((0,), (0,)), ((), ())),
                             preferred_element_type=F32)
        acc_sc[h] = alpha * acc_sc[h] + pv
        m_sc[h] = m_next

    for h in range(N_HEADS):
        scores(h, 0, st0_sc)

    def step(koff, cur, nxt, *, nk=tk, vis=None, knext=None, next_nk=tk):
        if knext is not None:
            scores(0, knext, nxt, next_nk)
        for h in range(N_HEADS):
            if h + 1 < N_HEADS and knext is not None:
                scores(h + 1, knext, nxt, next_nk)
            softmax_pv(h, koff, cur, vis, nk)

    at = lambda c: pl.multiple_of(c * tk, tk)

    def chunk_pair(i, carry):
        step(at(2 * i), st0_sc, st1_sc, knext=at(2 * i + 1))
        step(at(2 * i + 1), st1_sc, st0_sc, knext=at(2 * i + 2))
        return carry

    dk = min(tk, tq)
    n_diag = tq // dk
    qc = lax.broadcasted_iota(jnp.int32, (dk, rows), 1)
    if maps == 2:
        qc = jnp.where(qc >= tq, qc - tq, qc)
    krow = lax.broadcasted_iota(jnp.int32, (dk, rows), 0)
    kdiag = at(n_full)

    def diagonal(first_buf, second_buf):
        bufs = (first_buf, second_buf)
        for d in range(n_diag):
            vis = ((krow + d * dk) >> 6) <= (qc >> 6)
            knext = pl.multiple_of(kdiag + (d + 1) * dk, dk) if d + 1 < n_diag else None
            step(pl.multiple_of(kdiag + d * dk, dk), bufs[d % 2], bufs[(d + 1) % 2], nk=dk, vis=vis,
                 knext=knext, next_nk=dk)

    if dk != tk:
        step(0, st0_sc, st1_sc, knext=kdiag, next_nk=dk)
        diagonal(st1_sc, st0_sc)
    elif even_chunks:
        lax.fori_loop(0, n_full // 2, chunk_pair, 0)
        diagonal(st0_sc, st1_sc)
    else:
        lax.fori_loop(0, n_full // 2, chunk_pair, 0)
        odd = n_full % 2 == 1

        @pl.when(odd)
        def _odd():
            step(at(n_full - 1), st0_sc, st1_sc, knext=kdiag)
            diagonal(st1_sc, st0_sc)

        @pl.when(jnp.logical_not(odd))
        def _even():
            diagonal(st0_sc, st1_sc)

    if maps == 2:
        lv = lamv_ref[...]
        lam = (jnp.exp(jnp.sum(lv[0:1] * lv[1:2], axis=-1, keepdims=True))
               - jnp.exp(jnp.sum(lv[2:3] * lv[3:4], axis=-1, keepdims=True)) + lam_init)
    tb = min(tq, 2 * LANES)
    eye_q = (lax.broadcasted_iota(jnp.int32, (tb, tb), 0)
             == lax.broadcasted_iota(jnp.int32, (tb, tb), 1)).astype(BF16)
    for h in range(N_HEADS):
        sl = slice(h * LANES, (h + 1) * LANES)
        ot = acc_sc[h] * (1.0 / l_sc[h])
        if maps == 2:
            ot = ot[:, :tq] - lam * ot[:, tq:]
            ms = jnp.mean(ot * ot, axis=0, keepdims=True)
            ot = ((ot * lax.rsqrt(ms + EPS)) * gsub_ref[...]) * (1.0 - lam_init)
        ot = ot.astype(BF16)
        for j in range(tq // tb):
            o = lax.dot_general(eye_q, ot[:, j * tb:(j + 1) * tb], (((1,), (1,)), ((), ())),
                                preferred_element_type=F32)
            o_ref[j * tb:(j + 1) * tb, sl] = o.astype(o_ref.dtype)


SCORE_BOUND_MAX = 40.0


def _attention(q, k, v, lamv, gsub, *, maps, q_off, lam_init, tq, tk, bounded=False):
    b, sq, _ = q.shape
    sk = k.shape[1]
    assert sq % tq == 0 and tq % CHUNK == 0 and tk % CHUNK == 0 and sk == q_off + sq
    if tk > tq:
        assert q_off == tk and sq == tq
    else:
        assert tq % tk == 0 and q_off % tk == 0
    rows = maps * tq
    kern = functools.partial(_attn_kernel, maps=maps, tq=tq, tk=tk, q_off=q_off, lam_init=lam_init,
                             bounded=bounded)
    kv_spec = pl.BlockSpec((None, sk, D_MODEL), lambda bi, qi: (bi, 0, 0))
    return pl.pallas_call(
        kern, grid=(b, sq // tq),
        in_specs=[_const_spec((4, LANES)), _const_spec((LANES, 1)),
                  pl.BlockSpec((None, tq, D_MODEL), lambda bi, qi: (bi, qi, 0)), kv_spec, kv_spec],
        out_specs=pl.BlockSpec((None, tq, D_MODEL), lambda bi, qi: (bi, qi, 0)),
        out_shape=jax.ShapeDtypeStruct((b, sq, D_MODEL), BF16),
        scratch_shapes=[pltpu.VMEM((N_HEADS, LANES, rows), BF16), pltpu.VMEM((N_HEADS, 1, rows), F32),
                        pltpu.VMEM((N_HEADS, 1, rows), F32), pltpu.VMEM((N_HEADS, LANES, rows), F32),
                        pltpu.VMEM((N_HEADS, tk, rows), F32), pltpu.VMEM((N_HEADS, min(tk, tq), rows), F32)],
        compiler_params=pltpu.CompilerParams(dimension_semantics=("parallel", "arbitrary"),
                                             vmem_limit_bytes=VMEM_LIMIT),
        name=("attn_diff" if maps == 2 else "attn_mla") + ("_bounded" if bounded else ""),
    )(lamv, gsub, q, k, v)


def _merge_kernel(x_ref, oa_ref, ob_ref, ga_ref, gb_ref, wout_ref, gffn_ref, wrh_ref, wrl_ref, br_ref,
                  h_ref, hn_ref, route_ref, counts_ref, cnt_sc):
    merged = (ga_ref[...].astype(F32) * oa_ref[...].astype(F32)
              + gb_ref[...].astype(F32) * ob_ref[...].astype(F32))
    h = x_ref[...] + _dot(merged.astype(BF16), wout_ref[...])
    h_ref[...] = h
    hn = _row_rms(h, gffn_ref[...])
    hn_ref[...] = hn

    hi = hn.astype(BF16)
    lo = (hn - hi.astype(F32)).astype(BF16)
    logits = _dot(hi, wrh_ref[...]) + (_dot(lo, wrh_ref[...]) + _dot(hi, wrl_ref[...])) + br_ref[...]
    lane = lax.broadcasted_iota(jnp.int32, logits.shape, 1).astype(F32)
    neg = jnp.full_like(logits, -jnp.inf)
    big = jnp.full_like(logits, 1e9)

    def first_argmax(vals):
        vmax = jnp.max(vals, axis=1, keepdims=True)
        return vmax, jnp.min(jnp.where(vals == vmax, lane, big), axis=1, keepdims=True)

    gmask = lane < N_GROUPS
    gmax, gsel = first_argmax(jnp.where(gmask, logits, neg))
    p_top = 1.0 / jnp.sum(jnp.where(gmask, jnp.exp(logits - gmax), 0.0), axis=1, keepdims=True)
    e_lo = N_GROUPS + gsel * EXP_PER_GROUP
    emask = jnp.logical_and(lane >= e_lo, lane < e_lo + EXP_PER_GROUP)
    le = jnp.where(emask, logits, neg)
    v1, i1 = first_argmax(le)
    v2, i2 = first_argmax(jnp.where(lane == i1, neg, le))
    t = jnp.exp(v2 - v1)
    w1 = p_top / (1.0 + t)
    w2 = p_top * t / (1.0 + t)
    e1 = i1 - N_GROUPS
    e2 = i2 - N_GROUPS

    @pl.when(pl.program_id(0) == 0)
    def _zero_counts():
        cnt_sc[...] = jnp.zeros(cnt_sc.shape, F32)

    tm = logits.shape[0]
    oh1 = lane == e1
    oh2 = lane == e2
    earlier = (lax.broadcasted_iota(jnp.int32, (tm, tm), 1)
               < lax.broadcasted_iota(jnp.int32, (tm, tm), 0)).astype(BF16)
    before1 = _dot(earlier, oh1.astype(BF16))
    before2 = _dot(earlier, oh2.astype(BF16))
    base = cnt_sc[...]
    c1 = jnp.sum(oh1.astype(F32), axis=0, keepdims=True)
    c2 = jnp.sum(oh2.astype(F32), axis=0, keepdims=True)
    rank1 = jnp.sum(jnp.where(oh1, base + before1, 0.0), axis=1, keepdims=True)
    rank2 = jnp.sum(jnp.where(oh2, (base + c1) + before2, 0.0), axis=1, keepdims=True)
    total = base + (c1 + c2)
    cnt_sc[...] = total
    counts_ref[...] = total

    route = jnp.where(lane == 0, e1, 0.0)
    route = jnp.where(lane == 1, e2, route)
    route = jnp.where(lane == 2, w1, route)
    route = jnp.where(lane == 3, w2, route)
    route = jnp.where(lane == 4, rank1, route)
    route = jnp.where(lane == 5, rank2, route)
    route_ref[...] = route


def _merge(x, oa, ob, ga, gb, wp):
    n = x.shape[0]
    tm = 512 if n % 512 == 0 else 256
    assert n % tm == 0

    def tok(width):
        return pl.BlockSpec((tm, width), lambda i: (i, 0))

    return pl.pallas_call(
        _merge_kernel, grid=(n // tm,),
        in_specs=[tok(D_MODEL)] * 5 + [_const_spec((D_MODEL, D_MODEL)), _const_spec((1, D_MODEL)),
                                       _const_spec((D_MODEL, LANES)), _const_spec((D_MODEL, LANES)),
                                       _const_spec((1, LANES))],
        out_specs=[tok(D_MODEL), tok(D_MODEL), tok(LANES), pl.BlockSpec((1, LANES), lambda i: (0, 0))],
        out_shape=[jax.ShapeDtypeStruct((n, D_MODEL), F32), jax.ShapeDtypeStruct((n, D_MODEL), F32),
                   jax.ShapeDtypeStruct((n, LANES), F32), jax.ShapeDtypeStruct((1, LANES), F32)],
        scratch_shapes=[pltpu.VMEM((1, LANES), F32)],
        compiler_params=pltpu.CompilerParams(dimension_semantics=("arbitrary",), vmem_limit_bytes=VMEM_LIMIT),
        name="merge",
    )(x, oa, ob, ga, gb, wp["w_out"], wp["g_ffn"], wp["w_r_hi"], wp["w_r_lo"], wp["b_r"])


MOE_TILE = 512


def _gather_rows(idx_ref, n_rows, src_hbm, dst_ref, sem, *, unrolled):
    def start(r):
        pltpu.make_async_copy(src_hbm.at[pl.ds(idx_ref[0, 0, r], 1), :], dst_ref.at[pl.ds(r, 1), :], sem).start()

    if unrolled:
        for r in range(n_rows):
            start(r)
    else:
        def body(r, carry):
            start(r)
            return carry
        lax.fori_loop(0, n_rows, body, 0, unroll=8)


def _wait_rows(n_rows, src_hbm, dst_ref, sem):
    pltpu.make_async_copy(src_hbm.at[pl.ds(0, n_rows), :], dst_ref, sem).wait()


DISPATCH_TILE = 512


DISPATCH_SLOTS = 3


def _dispatch_kernel(pos_ref, hn_hbm, xs_hbm, buf, load_sem, scat_sem):
    i = pl.program_id(0)
    n = pl.num_programs(0)
    td = DISPATCH_TILE
    slot = i % DISPATCH_SLOTS

    def load(t, s):
        return pltpu.make_async_copy(hn_hbm.at[pl.ds(pl.multiple_of(t * td, td), td), :], buf.at[s], load_sem.at[s])

    def drain(s):
        pltpu.make_async_copy(xs_hbm.at[pl.ds(0, 2 * td), :], xs_hbm.at[pl.ds(0, 2 * td), :], scat_sem.at[s]).wait()

    @pl.when(i == 0)
    def _first_load():
        load(0, 0).start()

    @pl.when(i + 1 < n)
    def _next_load():
        nxt = (i + 1) % DISPATCH_SLOTS

        @pl.when(i >= 2)
        def _free_slot():
            drain(nxt)

        load(i + 1, nxt).start()

    load(i, slot).wait()
    for r in range(td):
        for c in range(2):
            dst = xs_hbm.at[pl.ds(pos_ref[0, 0, c * td + r], 1), :]
            pltpu.make_async_copy(buf.at[slot, pl.ds(r, 1), :], dst, scat_sem.at[slot]).start()

    @pl.when(i == n - 1)
    def _drain_all():
        @pl.when(i >= 2)
        def _():
            drain((i + 1) % DISPATCH_SLOTS)

        @pl.when(i >= 1)
        def _():
            drain((i + 2) % DISPATCH_SLOTS)

        drain(slot)


def _dispatch(hn, pos3):
    n = hn.shape[0]
    assert n % DISPATCH_TILE == 0
    return pl.pallas_call(
        _dispatch_kernel, grid=(n // DISPATCH_TILE,),
        in_specs=[pl.BlockSpec((1, 1, 2 * DISPATCH_TILE), lambda i: (i, 0, 0), memory_space=pltpu.SMEM),
                  pl.BlockSpec(memory_space=pl.ANY)],
        out_specs=pl.BlockSpec(memory_space=pl.ANY),
        out_shape=jax.ShapeDtypeStruct((2 * n, D_MODEL), F32),
        scratch_shapes=[pltpu.VMEM((DISPATCH_SLOTS, DISPATCH_TILE, D_MODEL), F32),
                        pltpu.SemaphoreType.DMA((DISPATCH_SLOTS,)), pltpu.SemaphoreType.DMA((DISPATCH_SLOTS,))],
        compiler_params=pltpu.CompilerParams(dimension_semantics=("arbitrary",)),
        name="dispatch",
    )(pos3, hn)


def _moe_kernel(vtile_ref, vexp_ref, vlo_ref, vhi_ref, vfirst_ref, xs_ref, wei_ref, weo_ref, y_ref):
    del vexp_ref
    v = pl.program_id(0)
    x = xs_ref[...].astype(BF16)
    ab = _dot(x, wei_ref[...])
    a = ab[:, :D_EXPERT]
    act = (a * _sigmoid(a)) * ab[:, D_EXPERT:]
    y = _dot(act.astype(BF16), weo_ref[...])

    @pl.when(vfirst_ref[v] == 1)
    def _first_visit():
        y_ref[...] = y

    @pl.when(vfirst_ref[v] == 0)
    def _later_visit():
        tile = y_ref.shape[0]
        row = vtile_ref[v] * tile + lax.broadcasted_iota(jnp.int32, (tile, 1), 0)
        mine = jnp.logical_and(row >= vlo_ref[v], row < vhi_ref[v])
        y_ref[...] = jnp.where(mine, y, y_ref[...])


def _moe(xs, visits, wp, tile):
    n_visits = visits[0].shape[0]
    grid_spec = pltpu.PrefetchScalarGridSpec(
        num_scalar_prefetch=5, grid=(n_visits,),
        in_specs=[pl.BlockSpec((tile, D_MODEL), lambda v, vt, ve, lo, hi, fi: (vt[v], 0)),
                  pl.BlockSpec((None, D_MODEL, 2 * D_EXPERT), lambda v, vt, ve, lo, hi, fi: (ve[v], 0, 0)),
                  pl.BlockSpec((None, D_EXPERT, D_MODEL), lambda v, vt, ve, lo, hi, fi: (ve[v], 0, 0))],
        out_specs=pl.BlockSpec((tile, D_MODEL), lambda v, vt, ve, lo, hi, fi: (vt[v], 0)),
    )
    return pl.pallas_call(
        _moe_kernel, grid_spec=grid_spec,
        out_shape=jax.ShapeDtypeStruct(xs.shape, F32),
        compiler_params=pltpu.CompilerParams(dimension_semantics=("arbitrary",), vmem_limit_bytes=VMEM_LIMIT),
        name="moe",
    )(*visits, xs, wp["w_exp_in"], wp["w_exp_out"])


def _routing_tables(route, counts, tile):
    e = route[:, 0:2].astype(jnp.int32)
    rank = route[:, 4:6].astype(jnp.int32)
    cnt = counts[0, :N_EXPERTS].astype(jnp.int32)
    end = jnp.cumsum(cnt)
    start = end - cnt
    ids = jnp.arange(N_EXPERTS, dtype=jnp.int32)
    pos = rank + jnp.sum(jnp.where(e[:, :, None] == ids[None, None, :], start[None, None, :], 0), axis=2)

    n_visits = (2 * route.shape[0]) // tile + N_EXPERTS - 1
    first_tile = start // tile
    n_vis = jnp.where(cnt > 0, (end - 1) // tile - first_tile + 1, 0)
    vis_end = jnp.cumsum(n_vis)
    vis_start = vis_end - n_vis
    v = jnp.minimum(jnp.arange(n_visits, dtype=jnp.int32), vis_end[-1] - 1)
    v_exp = jnp.sum((v[:, None] >= vis_end[None, :]).astype(jnp.int32), axis=1)
    pick = lambda table: jnp.sum(jnp.where(v_exp[:, None] == ids[None, :], table[None, :], 0), axis=1)
    v_tile = pick(first_tile) + (v - pick(vis_start))
    v_first = jnp.concatenate([jnp.ones((1,), jnp.int32), (v_tile[1:] != v_tile[:-1]).astype(jnp.int32)])
    visits = tuple(a.astype(jnp.int32) for a in (v_tile, v_exp, pick(start), pick(end), v_first))
    return pos, visits


COMB_TILE = 512


def _combine_kernel(pos_ref, posn_ref, route_ref, h_ref, pe_ref, y_hbm, gple_ref, wg_ref, wp_ref,
                    o_ref, ybuf, sem):
    i = pl.program_id(0)
    n = pl.num_programs(0)
    slot = i % 2
    rows = 2 * COMB_TILE

    @pl.when(i == 0)
    def _first():
        _gather_rows(pos_ref, rows, y_hbm, ybuf.at[0], sem.at[0], unrolled=False)

    _wait_rows(rows, y_hbm, ybuf.at[slot], sem.at[slot])
    _gather_rows(posn_ref, rows, y_hbm, ybuf.at[1 - slot], sem.at[1 - slot], unrolled=True)
    route = route_ref[...]
    h = h_ref[...] + (route[:, 2:3] * ybuf[slot, 0:COMB_TILE, :] + route[:, 3:4] * ybuf[slot, COMB_TILE:rows, :])
    gate = _sigmoid(_dot(_row_rms(h, gple_ref[...]).astype(BF16), wg_ref[...]))
    o_ref[...] = h + gate * _dot(pe_ref[...].astype(BF16), wp_ref[...])

    @pl.when(i == n - 1)
    def _drain_extra():
        _wait_rows(rows, y_hbm, ybuf.at[1 - slot], sem.at[1 - slot])


def _combine(route, h, pe, y, pos3, wp):
    n = h.shape[0]
    assert n % COMB_TILE == 0 and COMB_TILE == DISPATCH_TILE
    n_tiles = n // COMB_TILE

    def tok(width):
        return pl.BlockSpec((COMB_TILE, width), lambda i: (i, 0))

    pos_spec = pl.BlockSpec((1, 1, 2 * COMB_TILE), lambda i: (i, 0, 0), memory_space=pltpu.SMEM)
    posn_spec = pl.BlockSpec((1, 1, 2 * COMB_TILE), lambda i: (jnp.minimum(i + 1, n_tiles - 1), 0, 0),
                             memory_space=pltpu.SMEM)
    return pl.pallas_call(
        _combine_kernel, grid=(n_tiles,),
        in_specs=[pos_spec, posn_spec, tok(LANES), tok(D_MODEL), tok(D_PLE), pl.BlockSpec(memory_space=pl.ANY),
                  _const_spec((1, D_MODEL)), _const_spec((D_MODEL, D_MODEL)), _const_spec((D_PLE, D_MODEL))],
        out_specs=tok(D_MODEL),
        out_shape=jax.ShapeDtypeStruct((n, D_MODEL), F32),
        scratch_shapes=[pltpu.VMEM((2, 2 * COMB_TILE, D_MODEL), F32), pltpu.SemaphoreType.DMA((2,))],
        compiler_params=pltpu.CompilerParams(dimension_semantics=("arbitrary",), vmem_limit_bytes=VMEM_LIMIT),
        name="combine",
    )(pos3, pos3, route, h, pe, y, wp["g_ple"], wp["w_ple_gate"], wp["w_ple_proj"])


def _rope_tables(pos, rot, offsets):
    half = rot // 2
    inv = ROPE_THETA ** (-jnp.arange(half, dtype=F32) * (2.0 / rot))
    lane = jnp.arange(LANES)
    rel = jnp.full((LANES,), -1)
    for o in offsets:
        rel = jnp.where((lane >= o) & (lane < o + rot), lane - o, rel)
    first, second = (rel >= 0) & (rel < half), rel >= half
    ang = pos.astype(F32)[:, None] * inv[jnp.maximum(rel, 0) % half][None, :]
    c, s = jnp.cos(ang), jnp.sin(ang)
    return jnp.stack([jnp.where(rel >= 0, c, 1.0), jnp.where(first, -s, 0.0), jnp.where(second, s, 0.0)])


def _prep_weights(l, g_mix, w_in, g_q_lat, w_uq, g_kv_lat, w_uk, w_uv, g_mla_qn, g_mla_qr, g_mla_kn, g_mla_kr,
                  g_diff_q, g_diff_k, lambda_q1, lambda_k1, lambda_q2, lambda_k2, g_diff_sub, w_out,
                  g_ffn, w_router_grp, b_router_grp, w_router_exp, b_router_exp, w_exp_in, w_exp_out,
                  g_ple, w_ple_gate, w_ple_proj):
    offs = [0]
    for sz in IN_SIZES:
        offs.append(offs[-1] + sz)
    wi = w_in[l]
    seg = [wi[:, offs[j]:offs[j + 1]] for j in range(len(IN_SIZES))]
    w_kr_pad = jnp.pad(seg[2], ((0, 0), (0, LANES - QK_ROPE)))
    w_main = jnp.concatenate([seg[0], seg[1], w_kr_pad] + seg[3:], axis=1).astype(BF16)

    hd = QK_NOPE + QK_ROPE
    uq = w_uq[l].reshape(Q_LORA, H_A, hd)
    uq_pad = jnp.concatenate([uq[:, :, QK_NOPE:], jnp.zeros((Q_LORA, H_A, LANES - hd), F32), uq[:, :, :QK_NOPE]],
                             axis=2).reshape(Q_LORA, H_A * LANES).astype(BF16)
    uk = w_uk[l].reshape(KV_LORA, H_A, QK_NOPE)
    uk_pad = jnp.concatenate([jnp.zeros((KV_LORA, H_A, LANES - QK_NOPE), F32), uk], axis=2)
    uk_pad = uk_pad.reshape(KV_LORA, H_A * LANES).astype(BF16)

    z32 = jnp.zeros((LANES - hd,), F32)
    gvecs = jnp.stack([
        jnp.concatenate([g_mla_qr[l], z32, g_mla_qn[l]]),
        jnp.concatenate([g_mla_kr[l], jnp.zeros((LANES - QK_ROPE,), F32)]),
        jnp.concatenate([g_diff_q[l], g_diff_q[l]]),
        jnp.concatenate([g_diff_k[l], g_diff_k[l]]),
        jnp.concatenate([jnp.zeros((LANES - QK_NOPE,), F32), g_mla_kn[l]]),
        jnp.zeros((LANES,), F32), jnp.zeros((LANES,), F32), jnp.zeros((LANES,), F32)])

    lane = jnp.arange(LANES)
    seg_a = jnp.where(lane < QK_ROPE, 0, jnp.where(lane < LANES - QK_NOPE, -1, 1))
    same_a = (seg_a[:, None] == seg_a[None, :]) & (seg_a[:, None] >= 0)
    bd_a = jnp.where(same_a, jnp.where(seg_a[:, None] == 0, 1.0 / QK_ROPE, 1.0 / QK_NOPE), 0.0)
    seg_b = lane // D_B
    bd_b = jnp.where(seg_b[:, None] == seg_b[None, :], 1.0 / D_B, 0.0)
    pair = lambda m: jnp.kron(jnp.eye(2, dtype=F32), m)
    bd = jnp.stack([pair(bd_a), pair(bd_b)]).astype(BF16)

    w_r = jnp.concatenate([w_router_grp[l], w_router_exp[l],
                           jnp.zeros((D_MODEL, LANES - N_GROUPS - N_EXPERTS), F32)], axis=1)
    w_r_hi = w_r.astype(BF16)
    w_r_lo = (w_r - w_r_hi.astype(F32)).astype(BF16)
    b_r = jnp.concatenate([b_router_grp[l], b_router_exp[l],
                           jnp.zeros((LANES - N_GROUPS - N_EXPERTS,), F32)]).reshape(1, LANES)

    amax = lambda g: jnp.max(jnp.abs(g[l]))
    bound_diff = 1.05 * DIFF_SCALE * LOG2E * D_B * amax(g_diff_q) * amax(g_diff_k)
    bound_mla = 1.05 * MLA_SCALE * LOG2E * (
        jnp.sqrt(QK_NOPE * amax(g_mla_qn) ** 2 + QK_ROPE * amax(g_mla_qr) ** 2)
        * jnp.sqrt(QK_NOPE * amax(g_mla_kn) ** 2 + QK_ROPE * amax(g_mla_kr) ** 2))

    pad64 = jnp.zeros((LANES - D_B,), F32)
    lamv = jnp.stack([jnp.concatenate([v[l], pad64]) for v in (lambda_q1, lambda_k1, lambda_q2, lambda_k2)])

    return {
        "g_mix": g_mix[l].reshape(1, D_MODEL), "w_main": w_main,
        "g_q_lat": g_q_lat[l].reshape(1, Q_LORA), "w_uq": uq_pad,
        "g_kv_lat": g_kv_lat[l].reshape(1, KV_LORA), "w_uk": uk_pad, "w_uv": w_uv[l].astype(BF16),
        "gvecs": gvecs, "bd": bd, "lamv": lamv, "bound_mla": bound_mla, "bound_diff": bound_diff, "g_diff_sub": g_diff_sub[l].reshape(V_B, 1),
        "w_out": w_out[l].astype(BF16), "g_ffn": g_ffn[l].reshape(1, D_MODEL),
        "w_r_hi": w_r_hi, "w_r_lo": w_r_lo, "b_r": b_r,
        "w_exp_in": w_exp_in[l].astype(BF16), "w_exp_out": w_exp_out[l].astype(BF16),
        "g_ple": g_ple[l].reshape(1, D_MODEL), "w_ple_gate": w_ple_gate[l].astype(BF16),
        "w_ple_proj": w_ple_proj[l].astype(BF16),
    }


def _layer(x, pe, pos, past, wp, lam_init):
    b, s, _ = x.shape
    tab_a = _rope_tables(pos, QK_ROPE, (0,))
    tab_b = _rope_tables(pos, ROT_B, (0, D_B))
    if s < 512:
        outs = _proj(x.reshape(1, b * s, D_MODEL), jnp.tile(tab_a, (1, b, 1)), jnp.tile(tab_b, (1, b, 1)), wp)
        outs = [o.reshape(b, s, o.shape[-1]) for o in outs]
    else:
        outs = _proj(x, tab_a, tab_b, wp)
    q_cat, ckv, kr, dq, dk32, dk16, dv32, dv16, ga, gb, krope = outs

    if past is None:
        ckv_all, kr_all, dk_all, dv_all = ckv, kr, dk16, dv16
        q_off = 0
        tq = tk = min(256, s)
    else:
        ckv_p, kr_p, dk_p, dv_p = past
        past_len = ckv_p.shape[1]
        ckv_all = jnp.concatenate([ckv_p, ckv], axis=1)
        kr_all = jnp.concatenate([jnp.pad(kr_p, ((0, 0), (0, 0), (0, LANES - QK_ROPE))), kr], axis=1)
        dk_all = jnp.concatenate([dk_p.reshape(b, past_len, D_MODEL).astype(BF16), dk16], axis=1)
        dv_all = jnp.concatenate([dv_p.reshape(b, past_len, D_MODEL).astype(BF16), dv16], axis=1)
        q_off = past_len
        tq, tk = s, past_len
    sk = ckv_all.shape[1]

    k_cat, v_a = _kv_up(ckv_all.reshape(b * sk, KV_LORA), kr_all.reshape(b * sk, LANES), wp)

    def attend(q, k, v, maps, score_bound):
        run = functools.partial(_attention, maps=maps, q_off=q_off, lam_init=lam_init, tq=tq, tk=tk)
        args = (q, k, v, wp["lamv"], wp["g_diff_sub"])
        if past is not None:
            return run(*args)
        return lax.cond(score_bound <= SCORE_BOUND_MAX,
                        lambda *a: run(*a, bounded=True), lambda *a: run(*a, bounded=False), *args)

    o_a = attend(q_cat, k_cat.reshape(b, sk, D_MODEL), v_a.reshape(b, sk, D_MODEL), 1, wp["bound_mla"])
    o_b = attend(dq, dk_all, dv_all, 2, wp["bound_diff"])

    n = b * s
    flat = lambda a: a.reshape(n, a.shape[-1])
    h, hn, route, counts = _merge(flat(x), flat(o_a), flat(o_b), flat(ga), flat(gb), wp)

    moe_tile = MOE_TILE if 2 * n >= 16 * MOE_TILE else MOE_TILE // 4
    slot_of, visits = _routing_tables(route, counts, moe_tile)
    slots3 = slot_of.reshape(n // COMB_TILE, COMB_TILE, 2).transpose(0, 2, 1).reshape(n // COMB_TILE, 1, 2 * COMB_TILE)
    y = _moe(_dispatch(hn, slots3), visits, wp, moe_tile)
    out = _combine(route, h, flat(pe), y, slots3, wp)

    return (out.reshape(b, s, D_MODEL),
            (ckv, krope, dk32.reshape(b, s, H_B, 2, D_B), dv32.reshape(b, s, H_B, V_B)))


def kernel(x_prompt, x_sample, p_prompt, p_sample, cache_mla_ckv, cache_mla_krope, cache_diff_k, cache_diff_v,
           g_mix, w_in, g_q_lat, w_uq, g_kv_lat, w_uk, w_uv, g_mla_qn, g_mla_qr, g_mla_kn, g_mla_kr,
           g_diff_q, g_diff_k, lambda_q1, lambda_k1, lambda_q2, lambda_k2, g_diff_sub, w_out,
           g_ffn, w_router_grp, b_router_grp, w_router_exp, b_router_exp, w_exp_in, w_exp_out,
           g_ple, w_ple_gate, w_ple_proj):
    depth = w_in.shape[0]
    pos_p = jnp.arange(x_prompt.shape[1], dtype=jnp.int32)
    pos_s = cache_mla_ckv.shape[2] + jnp.arange(x_sample.shape[1], dtype=jnp.int32)
    hp, hs = x_prompt, x_sample
    st_p, st_s = [], []
    for l in range(depth):
        wp = _prep_weights(l, g_mix, w_in, g_q_lat, w_uq, g_kv_lat, w_uk, w_uv, g_mla_qn, g_mla_qr, g_mla_kn,
                           g_mla_kr, g_diff_q, g_diff_k, lambda_q1, lambda_k1, lambda_q2, lambda_k2, g_diff_sub,
                           w_out, g_ffn, w_router_grp, b_router_grp, w_router_exp, b_router_exp, w_exp_in,
                           w_exp_out, g_ple, w_ple_gate, w_ple_proj)
        lam_init = 0.8 - 0.6 * math.exp(-0.3 * l)
        hp, sp = _layer(hp, p_prompt[l], pos_p, None, wp, lam_init)
        hs, ss = _layer(hs, p_sample[l], pos_s,
                        (cache_mla_ckv[l], cache_mla_krope[l], cache_diff_k[l], cache_diff_v[l]), wp, lam_init)
        st_p.append(sp)
        st_s.append(ss)
    stack = lambda sts, j: jnp.stack([st[j] for st in sts])
    return (hp, hs,
            stack(st_p, 0), stack(st_p, 1), stack(st_p, 2), stack(st_p, 3),
            stack(st_s, 0), stack(st_s, 1), stack(st_s, 2), stack(st_s, 3))
```
